```python
import jax, jax.numpy as jnp
from jax import lax
import numpy as np

D_MODEL = 2048
BATCH = 2
SEQ = 4096
DEPTH = 4
DEC_BATCH = 8
DEC_SEQ = 8
PAST_LEN = 16384
PAGE_SIZE = 128

HEAD_DIM = 128
H_A = 6
KV_A = 2
IDX_H = 16
IDX_DIM = 128
TOPK_MAX = 256
CHUNK = 128
G_B = 4
W_B = 512
H_C = 6
QBLOCK = 128
N_EXPERTS = 64
TOP_K = 6
EXPERT_DIM = 512
SHARED_DIM = 512
ROUTED_SCALE = 2.5
EXPERT_BLOCK = 64
ALPHA = (2 * DEPTH) ** 0.25
BETA = (8 * DEPTH) ** -0.25
LN_EPS = 1e-5
A_W = H_A * HEAD_DIM
C_W = H_C * HEAD_DIM
IN_SIZES = (A_W, KV_A * HEAD_DIM, KV_A * HEAD_DIM, IDX_H * IDX_DIM, IDX_DIM, IDX_H,
            W_B, W_B, C_W, C_W, C_W, 3 * D_MODEL)
D_IN = sum(IN_SIZES)

kernel_name = 'hybrid_dsa_gmlp_stickbreak_moe_step'

F32 = jnp.float32


def _layer_norm(x, g, b):
    xf = x.astype(F32)
    xc = xf - jnp.mean(xf, axis=-1, keepdims=True)
    var = jnp.mean(xc * xc, axis=-1, keepdims=True)
    return (xc * lax.rsqrt(var + LN_EPS) * g.astype(F32) + b.astype(F32)).astype(x.dtype)


def _split_proj(h):
    B, T, _ = h.shape
    offs, acc = [], 0
    for s in IN_SIZES[:-1]:
        acc += s
        offs.append(acc)
    q_a, k_a, v_a, iq, ik, iw, u_b, v_b, q_c, k_c, v_c, gates = jnp.split(h, offs, axis=-1)
    heads = lambda a, n: a.reshape(B, T, n, -1)
    return (heads(q_a, H_A), heads(k_a, KV_A), heads(v_a, KV_A), heads(iq, IDX_H), ik, iw,
            u_b, v_b, heads(q_c, H_C), heads(k_c, H_C), heads(v_c, H_C), gates)


def _to_blocks(a):
    B, S = a.shape[:2]
    return jnp.moveaxis(a.reshape(B, S // QBLOCK, QBLOCK, *a.shape[2:]), 1, 0)


def _from_blocks(a):
    nb, B, qb = a.shape[:3]
    return jnp.moveaxis(a, 0, 1).reshape(B, nb * qb, *a.shape[3:])


def _take_rows(rows, idx):
    return jax.vmap(lambda r, i: r[i])(rows, idx)


def _indexer_scores(iq, iw, ik):
    s = jnp.einsum('bthd,bld->bthl', iq.astype(F32), ik.astype(F32)) * (IDX_DIM ** -0.5)
    return jnp.einsum('bthl,bth->btl', jax.nn.relu(s), iw.astype(F32) * (IDX_H ** -0.5))


def _select(scores, q_pos, n_keys):
    causal = jnp.arange(n_keys)[None, :] <= q_pos[:, None]
    topk = min(TOPK_MAX, n_keys // 4)
    _, idx = lax.top_k(jnp.where(causal[None], scores, -jnp.inf), topk)
    return idx, idx <= q_pos[None, :, None]


def _sparse_attend(q, k_sel, v_sel, valid):
    B, T = q.shape[:2]
    qg = q.reshape(B, T, KV_A, H_A // KV_A, HEAD_DIM).astype(F32)
    s = jnp.einsum('btgrd,btkgd->btgrk', qg, k_sel.astype(F32)) * (HEAD_DIM ** -0.5)
    p = jax.nn.softmax(jnp.where(valid[:, :, None, None, :], s, -jnp.inf), axis=-1)
    o = jnp.einsum('btgrk,btkgd->btgrd', p, v_sel.astype(F32))
    return o.reshape(B, T, A_W).astype(q.dtype)


def _dsa_prompt(q, k, v, iq, ik, iw):
    S = q.shape[1]

    def block(args):
        qb, iqb, iwb, start = args
        q_pos = start + jnp.arange(QBLOCK)
        idx, valid = _select(_indexer_scores(iqb, iwb, ik), q_pos, S)
        return _sparse_attend(qb, _take_rows(k, idx), _take_rows(v, idx), valid)

    out = lax.map(block, (_to_blocks(q), _to_blocks(iq), _to_blocks(iw),
                          jnp.arange(S // QBLOCK) * QBLOCK))
    return _from_blocks(out)


def _dsa_sample(q, k_new, v_new, iq, ik_new, iw, cache_k_a, cache_v_a, cache_idx_k, page_table, l):
    DB, T = q.shape[:2]
    past = page_table.shape[1] * PAGE_SIZE
    past_ik = cache_idx_k[l, page_table].reshape(DB, past, IDX_DIM)
    ik_all = jnp.concatenate([past_ik, ik_new], axis=1)
    q_pos = past + jnp.arange(T)
    idx, valid = _select(_indexer_scores(iq, iw, ik_all), q_pos, past + T)
    is_past = (idx < past)[..., None, None]
    pidx = jnp.minimum(idx, past - 1)
    phys = page_table[jnp.arange(DB)[:, None, None], pidx // PAGE_SIZE]
    off = pidx % PAGE_SIZE
    nidx = jnp.clip(idx - past, 0, T - 1)
    k_sel = jnp.where(is_past, cache_k_a[l, phys, off], _take_rows(k_new, nidx))
    v_sel = jnp.where(is_past, cache_v_a[l, phys, off], _take_rows(v_new, nidx))
    return _sparse_attend(q, k_sel, v_sel, valid)


def _stick_break(q, k, v, q_pos, k_pos):
    B, Tq = q.shape[:2]
    z = jnp.einsum('bqhd,bkhd->bhqk', q.astype(F32), k.astype(F32)) * (HEAD_DIM ** -0.5)
    mask = k_pos[None, :] < q_pos[:, None]
    log_keep = jnp.where(mask, jax.nn.log_sigmoid(-z), 0.0)
    later = lax.cumsum(log_keep, axis=3, reverse=True) - log_keep
    a = jnp.where(mask, jnp.exp(jax.nn.log_sigmoid(z) + later), 0.0)
    o = jnp.einsum('bhqk,bkhd->bqhd', a, v.astype(F32))
    return o.reshape(B, Tq, C_W).astype(q.dtype)


def _stick_prompt(q, k, v):
    S = q.shape[1]
    k_pos = jnp.arange(S)

    def block(args):
        qb, start = args
        return _stick_break(qb, k, v, start + jnp.arange(QBLOCK), k_pos)

    return _from_blocks(lax.map(block, (_to_blocks(q), jnp.arange(S // QBLOCK) * QBLOCK)))


def _stick_sample(q, k_new, v_new, cache_k_c, cache_v_c, page_table, l):
    DB, T = q.shape[:2]
    past = page_table.shape[1] * PAGE_SIZE
    k_all = jnp.concatenate([cache_k_c[l, page_table].reshape(DB, past, H_C, HEAD_DIM), k_new], axis=1)
    v_all = jnp.concatenate([cache_v_c[l, page_table].reshape(DB, past, H_C, HEAD_DIM), v_new], axis=1)
    return _stick_break(q, k_all, v_all, past + jnp.arange(T), jnp.arange(past + T))


def _chunk_gmlp(u, v, w_s, b_s, g, b):
    B, T, _ = u.shape
    tp = -(-T // CHUNK) * CHUNK
    vn = _layer_norm(jax.nn.gelu(v, approximate=False), g, b)
    vp = jnp.pad(vn, ((0, 0), (0, tp - T), (0, 0))).reshape(B, tp // CHUNK, CHUNK, G_B, W_B // G_B)
    w = w_s * jnp.tril(jnp.ones((CHUNK, CHUNK), w_s.dtype))
    mixed = jnp.einsum('gts,bcsgd->bctgd', w, vp) + b_s.T[:, :, None]
    mixed = mixed.reshape(B, tp, W_B)[:, :T]
    return jax.nn.gelu(u, approximate=False) * mixed, vn


def _swiglu(x, wg, wu, wd):
    return (jax.nn.silu(x @ wg) * (x @ wu)) @ wd


def _grouped_experts(x, idx, wts, w_gate, w_up, w_down):
    M, D = x.shape
    E = w_gate.shape[0]
    n_assign = M * TOP_K
    flat_e = idx.reshape(-1)
    flat_tok = jnp.arange(n_assign, dtype=jnp.int32) // TOP_K
    order = jnp.argsort(flat_e)
    se = flat_e[order]
    counts = jnp.zeros((E,), jnp.int32).at[flat_e].add(1)
    padded = (counts + EXPERT_BLOCK - 1) // EXPERT_BLOCK * EXPERT_BLOCK
    pend = jnp.cumsum(padded)
    dest = (pend - padded)[se] + jnp.arange(n_assign, dtype=jnp.int32) - (jnp.cumsum(counts) - counts)[se]
    n_blocks = -(-(n_assign + E * (EXPERT_BLOCK - 1)) // EXPERT_BLOCK)
    rows = n_blocks * EXPERT_BLOCK
    row_tok = jnp.full((rows,), M, jnp.int32).at[dest].set(flat_tok[order])
    row_w = jnp.zeros((rows,), wts.dtype).at[dest].set(wts.reshape(-1)[order])
    block_e = jnp.minimum(jnp.searchsorted(pend, jnp.arange(n_blocks) * EXPERT_BLOCK, side='right'), E - 1)
    x_pad = jnp.concatenate([x, jnp.zeros((1, D), x.dtype)], axis=0)

    def run(args):
        tok, e = args
        return _swiglu(x_pad[tok], w_gate[e], w_up[e], w_down[e])

    y = lax.map(run, (row_tok.reshape(n_blocks, EXPERT_BLOCK), block_e)).reshape(rows, D)
    return jax.ops.segment_sum(y * row_w[:, None], row_tok, num_segments=M + 1)[:M]


def _moe(x, w_router, b_router, w_gate, w_up, w_down, ws_gate, ws_up, ws_down):
    B, T, D = x.shape
    xf = x.reshape(B * T, D)
    scores = jax.nn.sigmoid((xf @ w_router).astype(F32))
    _, idx = lax.top_k(scores + b_router.astype(F32), TOP_K)
    wsel = jnp.take_along_axis(scores, idx, axis=-1)
    wsel = wsel / jnp.sum(wsel, axis=-1, keepdims=True) * ROUTED_SCALE
    routed = _grouped_experts(xf, idx, wsel.astype(x.dtype), w_gate, w_up, w_down)
    return (routed + _swiglu(xf, ws_gate, ws_up, ws_down)).reshape(B, T, D)


def setup_inputs(seed: int = 0) -> dict:
    key = jax.random.key(seed)
    ks = iter(jax.random.split(key, 32))
    nrm = lambda shape, scale: jax.random.normal(next(ks), shape, F32) * scale
    n_pages = PAST_LEN // PAGE_SIZE
    n_used = DEC_BATCH * n_pages
    n_pool = n_used + max(1, n_used // 4)
    L, D = DEPTH, D_MODEL
    x_prompt = nrm((BATCH, SEQ, D), 1.0)
    x_sample = nrm((DEC_BATCH, DEC_SEQ, D), 1.0)
    cache_k_a = nrm((L, n_pool, PAGE_SIZE, KV_A, HEAD_DIM), 1.0)
    cache_v_a = nrm((L, n_pool, PAGE_SIZE, KV_A, HEAD_DIM), 1.0)
    cache_idx_k = nrm((L, n_pool, PAGE_SIZE, IDX_DIM), 1.0)
    cache_k_c = nrm((L, n_pool, PAGE_SIZE, H_C, HEAD_DIM), 1.0)
    cache_v_c = nrm((L, n_pool, PAGE_SIZE, H_C, HEAD_DIM), 1.0)
    page_table = jax.random.permutation(next(ks), n_pool)[:n_used].reshape(DEC_BATCH, n_pages).astype(jnp.int32)
    return {
        'x_prompt': x_prompt,
        'x_sample': x_sample,
        'cache_k_a': cache_k_a,
        'cache_v_a': cache_v_a,
        'cache_idx_k': cache_idx_k,
        'cache_k_c': cache_k_c,
        'cache_v_c': cache_v_c,
        'page_table': page_table,
        'w_in': nrm((L, D, D_IN), D ** -0.5),
        'w_s': nrm((L, G_B, CHUNK, CHUNK), CHUNK ** -0.5),
        'b_s': 1.0 + nrm((L, G_B, CHUNK), 0.01),
        'ln_v_g': 1.0 + nrm((L, W_B), 0.01),
        'ln_v_b': nrm((L, W_B), 0.01),
        'w_pa': nrm((L, A_W, D), BETA * A_W ** -0.5),
        'w_pb': nrm((L, W_B, D), BETA * W_B ** -0.5),
        'w_pc': nrm((L, C_W, D), BETA * C_W ** -0.5),
        'w_out': nrm((L, D, D), BETA * D ** -0.5),
        'ln1_g': 1.0 + nrm((L, D), 0.01),
        'ln1_b': nrm((L, D), 0.01),
        'w_router': nrm((L, D, N_EXPERTS), D ** -0.5),
        'b_router': nrm((L, N_EXPERTS), 0.01),
        'w_gate': nrm((L, N_EXPERTS, D, EXPERT_DIM), D ** -0.5),
        'w_up': nrm((L, N_EXPERTS, D, EXPERT_DIM), D ** -0.5),
        'w_down': nrm((L, N_EXPERTS, EXPERT_DIM, D), BETA * EXPERT_DIM ** -0.5),
        'ws_gate': nrm((L, D, SHARED_DIM), D ** -0.5),
        'ws_up': nrm((L, D, SHARED_DIM), D ** -0.5),
        'ws_down': nrm((L, SHARED_DIM, D), BETA * SHARED_DIM ** -0.5),
        'ln2_g': 1.0 + nrm((L, D), 0.01),
        'ln2_b': nrm((L, D), 0.01),
    }


def reference(x_prompt, x_sample, cache_k_a, cache_v_a, cache_idx_k, cache_k_c, cache_v_c, page_table,
              w_in, w_s, b_s, ln_v_g, ln_v_b, w_pa, w_pb, w_pc, w_out, ln1_g, ln1_b,
              w_router, b_router, w_gate, w_up, w_down, ws_gate, ws_up, ws_down, ln2_g, ln2_b):
    def merge(o_a, o_b, o_c, gates, l):
        g_a, g_b, g_c = jnp.split(jax.nn.sigmoid(gates), 3, axis=-1)
        m = g_a * (o_a @ w_pa[l]) + g_b * (o_b @ w_pb[l]) + g_c * (o_c @ w_pc[l])
        return m @ w_out[l]

    def post_block(x, mix, l):
        x = _layer_norm(ALPHA * x + mix, ln1_g[l], ln1_b[l])
        ffn = _moe(x, w_router[l], b_router[l], w_gate[l], w_up[l], w_down[l],
                   ws_gate[l], ws_up[l], ws_down[l])
        return _layer_norm(ALPHA * x + ffn, ln2_g[l], ln2_b[l])

    xp, xs = x_prompt, x_sample
    st_p = [[] for _ in range(5)]
    st_s = [[] for _ in range(6)]
    for l in range(DEPTH):
        q_a, k_a, v_a, iq, ik, iw, u_b, v_b, q_c, k_c, v_c, gates = _split_proj(xp @ w_in[l])
        o_a = _dsa_prompt(q_a, k_a, v_a, iq, ik, iw)
        o_b, _ = _chunk_gmlp(u_b, v_b, w_s[l], b_s[l], ln_v_g[l], ln_v_b[l])
        o_c = _stick_prompt(q_c, k_c, v_c)
        for lst, a in zip(st_p, (k_a, v_a, ik, k_c, v_c)):
            lst.append(a)
        xp = post_block(xp, merge(o_a, o_b, o_c, gates, l), l)

        q_a, k_a, v_a, iq, ik, iw, u_b, v_b, q_c, k_c, v_c, gates = _split_proj(xs @ w_in[l])
        o_a = _dsa_sample(q_a, k_a, v_a, iq, ik, iw, cache_k_a, cache_v_a, cache_idx_k, page_table, l)
        o_b, vn = _chunk_gmlp(u_b, v_b, w_s[l], b_s[l], ln_v_g[l], ln_v_b[l])
        o_c = _stick_sample(q_c, k_c, v_c, cache_k_c, cache_v_c, page_table, l)
        for lst, a in zip(st_s, (k_a, v_a, ik, k_c, v_c, vn)):
            lst.append(a)
        xs = post_block(xs, merge(o_a, o_b, o_c, gates, l), l)

    new_k_a_p = jnp.stack(st_p[0])
    new_v_a_p = jnp.stack(st_p[1])
    new_ik_p = jnp.stack(st_p[2])
    new_k_c_p = jnp.stack(st_p[3])
    new_v_c_p = jnp.stack(st_p[4])
    new_k_a_s = jnp.stack(st_s[0])
    new_v_a_s = jnp.stack(st_s[1])
    new_ik_s = jnp.stack(st_s[2])
    new_k_c_s = jnp.stack(st_s[3])
    new_v_c_s = jnp.stack(st_s[4])
    new_vb_s = jnp.stack(st_s[5])
    return (xp, xs, new_k_a_p, new_v_a_p, new_ik_p, new_k_c_p, new_v_c_p,
            new_k_a_s, new_v_a_s, new_ik_s, new_k_c_s, new_v_c_s, new_vb_s)
```

```python
import functools

import jax
import jax.numpy as jnp
from jax import lax
from jax.experimental import pallas as pl
from jax.experimental.pallas import tpu as pltpu

F32 = jnp.float32
BF16 = jnp.bfloat16
I32 = jnp.int32

HEAD_DIM = 128
H_A = 6
KV_A = 2
IDX_H = 16
IDX_DIM = 128
TOPK_MAX = 256
CHUNK = 128
G_B = 4
W_B = 512
H_C = 6
QB = 128
TOP_K = 6
ROUTED_SCALE = 2.5
LN_EPS = 1e-5
A_W = H_A * HEAD_DIM
C_W = H_C * HEAD_DIM
LANES = 128
TM = 512
EXPERT_BLK = 256
ROUTE_TR = 256
DISPATCH_TD = 256
COMBINE_TC = 128
NEG = -1e30
INT_MIN = -2147483648
VMEM_LIMIT = 56 * 1024 * 1024


def _cp(sem, vmem=VMEM_LIMIT):
    return pltpu.CompilerParams(dimension_semantics=sem, vmem_limit_bytes=vmem)


def _layout(d_model):
    pieces = (("iq", IDX_H * IDX_DIM), ("g_a", d_model), ("g_b", d_model), ("g_c", d_model),
              ("k_a", KV_A * HEAD_DIM), ("q_a", A_W), ("q_c", C_W), ("k_c", C_W), ("v_c", C_W),
              ("v_a", KV_A * HEAD_DIM), ("u_b", W_B), ("v_b", W_B), ("ik", IDX_DIM), ("iw", LANES))
    src_order = (("q_a", A_W), ("k_a", KV_A * HEAD_DIM), ("v_a", KV_A * HEAD_DIM), ("iq", IDX_H * IDX_DIM),
                 ("ik", IDX_DIM), ("iw", IDX_H), ("u_b", W_B), ("v_b", W_B), ("q_c", C_W), ("k_c", C_W),
                 ("v_c", C_W), ("g_a", d_model), ("g_b", d_model), ("g_c", d_model))
    src, acc = {}, 0
    for name, w in src_order:
        src[name] = (acc, w)
        acc += w
    off, acc = {}, 0
    for name, w in pieces:
        assert acc % w == 0, (name, acc, w)
        off[name] = acc
        acc += w
    return pieces, src, off, acc


def _sortable(x):
    bits = lax.bitcast_convert_type(x, I32)
    return jnp.where(bits < 0, bits ^ 0x7FFFFFFF, bits)


def _gelu(x):
    return 0.5 * x * (1.0 + lax.erf(x * (2.0 ** -0.5)))


def _layer_norm(y, g, b):
    mu = jnp.mean(y, axis=-1, keepdims=True)
    yc = y - mu
    var = jnp.mean(yc * yc, axis=-1, keepdims=True)
    return yc * lax.rsqrt(var + LN_EPS) * g + b


def _dot(a, b):
    return jnp.dot(a, b, preferred_element_type=F32)


def _dot_nt(a, b):
    return lax.dot_general(a, b, (((1,), (1,)), ((), ())), preferred_element_type=F32)


def _log_sig_pair(z):
    sp = jnp.log1p(jnp.exp(-jnp.abs(z)))
    return -(jnp.maximum(z, 0.0) + sp), -(jnp.maximum(-z, 0.0) + sp)


def _suffix_matrix():
    r = lax.broadcasted_iota(I32, (LANES, 2 * LANES), 0)
    c = lax.broadcasted_iota(I32, (LANES, 2 * LANES), 1)
    return jnp.where((c >= LANES) | (r > c), 1.0, 0.0).astype(BF16)


def _suffix_sums(lk, u2):
    hi = lk.astype(BF16)
    lo = (lk - hi.astype(F32)).astype(BF16)
    cs = _dot(hi, u2) + _dot(lo, u2)
    return cs[:, :LANES], cs[:, LANES:]


def _matmul_kernel(x_ref, w_ref, o_ref):
    o_ref[...] = _dot(x_ref[...], w_ref[...])


def _in_proj(xb, w_r, layer):
    mt, d = xb.shape
    n = w_r.shape[2]
    tn = max(t for t in range(LANES, 1025, LANES) if n % t == 0)
    return pl.pallas_call(
        _matmul_kernel,
        grid=(n // tn, mt // TM),
        in_specs=[pl.BlockSpec((TM, d), lambda j, i: (i, 0)),
                  pl.BlockSpec((None, d, tn), lambda j, i: (layer, 0, j))],
        out_specs=pl.BlockSpec((TM, tn), lambda j, i: (i, j)),
        out_shape=jax.ShapeDtypeStruct((mt, n), F32),
        compiler_params=_cp(("arbitrary", "arbitrary")),
        name="in_proj",
    )(xb, w_r)


def _dsa_prompt_kernel(q_ref, iq_ref, iw_ref, k_ref, v_ref, ik_ref, o_ref,
                       iqt_s, key_s, vt_s, qb_s, ot_s, m_s, l_s, *, seq, topk):
    i = pl.program_id(1)
    n_kt = seq // QB

    @pl.when(i == 0)
    def _():
        def body(j, c):
            rows = pl.ds(pl.multiple_of(j * QB, QB), QB)
            for g in range(KV_A):
                vt_s[g, j] = v_ref[rows, g * HEAD_DIM:(g + 1) * HEAD_DIM].T.astype(BF16)
            return c
        lax.fori_loop(0, n_kt, body, 0)

    for h in range(IDX_H):
        iqt_s[:, h * QB:(h + 1) * QB] = iq_ref[:, h * IDX_DIM:(h + 1) * IDX_DIM].T.astype(BF16)
    qb_s[...] = q_ref[...].astype(BF16)
    iwt = iw_ref[...].T * ((IDX_DIM ** -0.5) * (IDX_H ** -0.5))

    row = lax.broadcasted_iota(I32, (QB, QB), 0)
    col = lax.broadcasted_iota(I32, (QB, QB), 1)

    def causal(j):
        return (j * QB + row) <= (i * QB + col)

    def score_body(j, c):
        ikt = ik_ref[pl.ds(pl.multiple_of(j * QB, QB), QB), :].astype(BF16)
        s_all = _dot(ikt, iqt_s[...])
        acc = jnp.zeros((QB, QB), F32)
        for h in range(IDX_H):
            acc = acc + jnp.maximum(s_all[:, h * QB:(h + 1) * QB], 0.0) * iwt[h:h + 1, :]
        key_s[j] = _sortable(jnp.where(causal(j), acc, -jnp.inf))
        return c

    lax.fori_loop(0, i + 1, score_body, 0)

    def bit_body(bi, tu):
        candu = tu | lax.shift_left(jnp.int32(1), 31 - bi)
        cand = candu ^ INT_MIN

        def count_body(j, cnt):
            return cnt + jnp.where(key_s[j] >= cand, 1.0, 0.0)

        cnt = lax.fori_loop(0, i + 1, count_body, jnp.zeros((QB, QB), F32))
        return jnp.where(jnp.sum(cnt, axis=0, keepdims=True) >= topk, candu, tu)

    thr = lax.fori_loop(0, 32, bit_body, jnp.zeros((1, QB), I32)) ^ INT_MIN

    m_s[...] = jnp.full(m_s.shape, NEG, F32)
    l_s[...] = jnp.zeros(l_s.shape, F32)
    ot_s[...] = jnp.zeros(ot_s.shape, F32)
    scale = HEAD_DIM ** -0.5

    def att_body(j, c):
        sel = (key_s[j] >= thr) & causal(j)
        rows = pl.ds(pl.multiple_of(j * QB, QB), QB)
        for g in range(KV_A):
            kt = k_ref[rows, g * HEAD_DIM:(g + 1) * HEAD_DIM].astype(BF16)
            vt = vt_s[g, j]
            for r in range(H_A // KV_A):
                h = g * (H_A // KV_A) + r
                st = _dot_nt(kt, qb_s[:, h * HEAD_DIM:(h + 1) * HEAD_DIM]) * scale
                m_old = m_s[h:h + 1, :]
                m_new = jnp.maximum(m_old, jnp.max(jnp.where(sel, st, NEG), axis=0, keepdims=True))
                p = jnp.where(sel, jnp.exp(st - m_new), 0.0)
                alpha = jnp.exp(m_old - m_new)
                l_s[h:h + 1, :] = alpha * l_s[h:h + 1, :] + jnp.sum(p, axis=0, keepdims=True)
                ot_s[h] = alpha * ot_s[h] + _dot(vt, p.astype(BF16))
                m_s[h:h + 1, :] = m_new
        return c

    lax.fori_loop(0, i + 1, att_body, 0)

    for h in range(H_A):
        o_ref[:, h * HEAD_DIM:(h + 1) * HEAD_DIM] = (ot_s[h] / l_s[h:h + 1, :]).T.astype(o_ref.dtype)


def _dsa_prompt(hp, off, batch, seq):
    nq = seq // QB
    topk = min(TOPK_MAX, seq // 4)
    kvw = KV_A * HEAD_DIM
    iqw = IDX_H * IDX_DIM
    return pl.pallas_call(
        functools.partial(_dsa_prompt_kernel, seq=seq, topk=topk),
        grid=(batch, nq),
        in_specs=[pl.BlockSpec((QB, A_W), lambda b, i: (b * nq + i, off["q_a"] // A_W)),
                  pl.BlockSpec((QB, iqw), lambda b, i: (b * nq + i, off["iq"] // iqw)),
                  pl.BlockSpec((QB, LANES), lambda b, i: (b * nq + i, off["iw"] // LANES)),
                  pl.BlockSpec((seq, kvw), lambda b, i: (b, off["k_a"] // kvw)),
                  pl.BlockSpec((seq, kvw), lambda b, i: (b, off["v_a"] // kvw)),
                  pl.BlockSpec((seq, IDX_DIM), lambda b, i: (b, off["ik"] // IDX_DIM))],
        out_specs=pl.BlockSpec((QB, A_W), lambda b, i: (b * nq + i, 0)),
        out_shape=jax.ShapeDtypeStruct((batch * seq, A_W), BF16),
        scratch_shapes=[pltpu.VMEM((IDX_DIM, IDX_H * QB), BF16),
                        pltpu.VMEM((nq, QB, QB), I32),
                        pltpu.VMEM((KV_A, nq, HEAD_DIM, QB), BF16),
                        pltpu.VMEM((QB, A_W), BF16),
                        pltpu.VMEM((H_A, HEAD_DIM, QB), F32),
                        pltpu.VMEM((8, QB), F32),
                        pltpu.VMEM((8, QB), F32)],
        compiler_params=_cp(("arbitrary", "arbitrary")),
        name="dsa_prompt",
    )(hp, hp, hp, hp, hp, hp)


def _stick_prompt_kernel(q_ref, k_ref, v_ref, o_ref):
    i = pl.program_id(2)
    q = q_ref[...].astype(BF16)
    u2 = _suffix_matrix()
    row = lax.broadcasted_iota(I32, (QB, QB), 0)
    col = lax.broadcasted_iota(I32, (QB, QB), 1)
    scale = HEAD_DIM ** -0.5

    def body(jj, carry):
        acc, tot = carry
        j = i - jj
        rows = pl.ds(pl.multiple_of(j * QB, QB), QB)
        z = _dot_nt(q, k_ref[rows, :].astype(BF16)) * scale
        mask = (j * QB + col) < (i * QB + row)
        lk, ls = _log_sig_pair(z)
        lk = jnp.where(mask, lk, 0.0)
        within, total = _suffix_sums(lk, u2)
        a = jnp.where(mask, jnp.exp(ls + within + tot), 0.0)
        acc = acc + _dot(a.astype(BF16), v_ref[rows, :].astype(BF16))
        return acc, tot + total

    zero = jnp.zeros((QB, QB), F32)
    acc, _ = lax.fori_loop(0, i + 1, body, (zero, zero))
    o_ref[...] = acc.astype(o_ref.dtype)


def _stick_prompt(hp, off, batch, seq):
    nq = seq // QB
    return pl.pallas_call(
        _stick_prompt_kernel,
        grid=(batch, H_C, nq),
        in_specs=[pl.BlockSpec((QB, HEAD_DIM), lambda b, h, i: (b * nq + i, off["q_c"] // HEAD_DIM + h)),
                  pl.BlockSpec((seq, HEAD_DIM), lambda b, h, i: (b, off["k_c"] // HEAD_DIM + h)),
                  pl.BlockSpec((seq, HEAD_DIM), lambda b, h, i: (b, off["v_c"] // HEAD_DIM + h))],
        out_specs=pl.BlockSpec((QB, HEAD_DIM), lambda b, h, i: (b * nq + i, h)),
        out_shape=jax.ShapeDtypeStruct((batch * seq, C_W), BF16),
        compiler_params=_cp(("arbitrary", "arbitrary", "arbitrary")),
        name="stick_prompt",
    )(hp, hp, hp)


def _gmlp_kernel(u_ref, v_ref, ws_ref, bst_ref, g_ref, b_ref, o_ref, vn_ref):
    vn = _layer_norm(_gelu(v_ref[...]), g_ref[...], b_ref[...])
    vn_ref[...] = vn
    gu = _gelu(u_ref[...])
    row = lax.broadcasted_iota(I32, (CHUNK, CHUNK), 0)
    col = lax.broadcasted_iota(I32, (CHUNK, CHUNK), 1)
    gw = W_B // G_B
    for g in range(G_B):
        w = jnp.where(row >= col, ws_ref[g], 0.0).astype(BF16)
        mixed = _dot(w, vn[:, g * gw:(g + 1) * gw].astype(BF16)) + bst_ref[:, g:g + 1]
        o_ref[:, g * gw:(g + 1) * gw] = (gu[:, g * gw:(g + 1) * gw] * mixed).astype(o_ref.dtype)


def _gmlp(u_src, v_src, u_blk, v_blk, n_chunks, w_s, b_s_t, g, b):
    return pl.pallas_call(
        _gmlp_kernel,
        grid=(n_chunks,),
        in_specs=[pl.BlockSpec((CHUNK, W_B), lambda c: (c, u_blk)),
                  pl.BlockSpec((CHUNK, W_B), lambda c: (c, v_blk)),
                  pl.BlockSpec((G_B, CHUNK, CHUNK), lambda c: (0, 0, 0)),
                  pl.BlockSpec((CHUNK, G_B), lambda c: (0, 0)),
                  pl.BlockSpec((1, W_B), lambda c: (0, 0)),
                  pl.BlockSpec((1, W_B), lambda c: (0, 0))],
        out_specs=[pl.BlockSpec((CHUNK, W_B), lambda c: (c, 0)),
                   pl.BlockSpec((CHUNK, W_B), lambda c: (c, 0))],
        out_shape=[jax.ShapeDtypeStruct((n_chunks * CHUNK, W_B), BF16),
                   jax.ShapeDtypeStruct((n_chunks * CHUNK, W_B), F32)],
        compiler_params=_cp(("arbitrary",)),
        name="gmlp",
    )(u_src, v_src, w_s, b_s_t, g, b)


def _dsa_sample_score_kernel(pt_ref, iq_ref, wrow_ref, iknew_ref, ik_ref, i_ref, thr_ref,
                             q2_s, key_s, *, n_pages, n_new, topk):
    del pt_ref
    p = pl.program_id(1)

    @pl.when(p == 0)
    def _():
        q2 = jnp.concatenate([iq_ref[:, h * IDX_DIM:(h + 1) * IDX_DIM] for h in range(IDX_H)], axis=0)
        q2_s[...] = q2.astype(BF16)

    wrow = wrow_ref[0:1, :] * ((IDX_DIM ** -0.5) * (IDX_H ** -0.5))

    def scores(keys):
        r = jnp.maximum(_dot_nt(keys.astype(BF16), q2_s[...]), 0.0) * wrow
        for sh in (64, 32, 16, 8):
            r = r + pltpu.roll(r, sh, 1)
        return r.T[0:n_new, :]

    it = scores(ik_ref[...])
    i_ref[p] = it
    key_s[p] = _sortable(it)

    @pl.when(p == n_pages - 1)
    def _():
        pad = jnp.zeros((LANES - n_new, IDX_DIM), F32)
        itn = scores(jnp.concatenate([iknew_ref[...], pad], axis=0))
        t = lax.broadcasted_iota(I32, (n_new, LANES), 0)
        s = lax.broadcasted_iota(I32, (n_new, LANES), 1)
        itn = jnp.where((s <= t) & (s < n_new), itn, -jnp.inf)
        i_ref[n_pages] = itn
        key_s[n_pages] = _sortable(itn)

        def bit_body(bi, tu):
            candu = tu | lax.shift_left(jnp.int32(1), 31 - bi)
            cand = candu ^ INT_MIN

            def count_body(j, cnt):
                return cnt + jnp.where(key_s[j] >= cand, 1.0, 0.0)

            cnt = lax.fori_loop(0, n_pages + 1, count_body, jnp.zeros((n_new, LANES), F32))
            return jnp.where(jnp.sum(cnt, axis=1, keepdims=True) >= topk, candu, tu)

        thr = lax.fori_loop(0, 32, bit_body, jnp.zeros((n_new, 1), I32)) ^ INT_MIN
        thr_ref[...] = jnp.broadcast_to(thr, thr_ref.shape)


def _dsa_sample_scores(hp, wrow, cache_idx_k, pt_flat, off, layer, mp, dec_batch, n_new, n_pages):
    topk = min(TOPK_MAX, (n_pages * LANES + n_new) // 4)
    iqw = IDX_H * IDX_DIM
    rb = mp // n_new
    grid_spec = pltpu.PrefetchScalarGridSpec(
        num_scalar_prefetch=1,
        grid=(dec_batch, n_pages),
        in_specs=[pl.BlockSpec((n_new, iqw), lambda b, p, pt: (rb + b, off["iq"] // iqw)),
                  pl.BlockSpec((n_new, LANES), lambda b, p, pt: (b, 0)),
                  pl.BlockSpec((n_new, IDX_DIM), lambda b, p, pt: (rb + b, off["ik"] // IDX_DIM)),
                  pl.BlockSpec((None, None, LANES, IDX_DIM), lambda b, p, pt: (layer, pt[b * n_pages + p], 0, 0))],
        out_specs=[pl.BlockSpec((None, n_pages + 1, n_new, LANES), lambda b, p, pt: (b, 0, 0, 0)),
                   pl.BlockSpec((n_new, LANES), lambda b, p, pt: (b, 0))],
        scratch_shapes=[pltpu.VMEM((IDX_H * n_new, IDX_DIM), BF16),
                        pltpu.VMEM((n_pages + 1, n_new, LANES), I32)])
    return pl.pallas_call(
        functools.partial(_dsa_sample_score_kernel, n_pages=n_pages, n_new=n_new, topk=topk),
        grid_spec=grid_spec,
        out_shape=[jax.ShapeDtypeStruct((dec_batch, n_pages + 1, n_new, LANES), F32),
                   jax.ShapeDtypeStruct((dec_batch * n_new, LANES), I32)],
        compiler_params=_cp(("arbitrary", "arbitrary")),
        name="dsa_sample_scores",
    )(pt_flat, hp, wrow, hp, cache_idx_k)


def _dsa_sample_attend_kernel(pt_ref, q_ref, i_ref, inew_ref, thr_ref, knew_ref, vnew_ref, k_ref, v_ref, o_ref,
                              qbd_s, o_s, m_s, l_s, *, n_pages, n_new):
    del pt_ref
    p = pl.program_id(1)
    rep = H_A // KV_A
    n_rows = H_A * n_new
    grp = lax.broadcasted_iota(I32, (n_rows, LANES), 0) // (rep * n_new)
    scale = HEAD_DIM ** -0.5

    @pl.when(p == 0)
    def _():
        qs = jnp.concatenate([q_ref[:, h * HEAD_DIM:(h + 1) * HEAD_DIM] for h in range(H_A)], axis=0)
        qbd_s[...] = jnp.concatenate([jnp.where(grp == g, qs, 0.0) for g in range(KV_A)], axis=1).astype(BF16)
        m_s[...] = jnp.full(m_s.shape, NEG, F32)
        l_s[...] = jnp.zeros(l_s.shape, F32)
        o_s[...] = jnp.zeros(o_s.shape, F32)

    def update(sel_q, kflat, vstack):
        sel = jnp.concatenate([jnp.where(sel_q, 1.0, 0.0)] * H_A, axis=0) > 0.5
        s = _dot_nt(qbd_s[...], kflat.astype(BF16)) * scale
        m_old = m_s[...]
        m_new = jnp.maximum(m_old, jnp.max(jnp.where(sel, s, NEG), axis=1, keepdims=True))
        pr = jnp.where(sel, jnp.exp(s - m_new), 0.0)
        alpha = jnp.exp(m_old - m_new)
        l_s[...] = alpha * l_s[...] + jnp.sum(pr, axis=1, keepdims=True)
        pbd = jnp.concatenate([jnp.where(grp == g, pr, 0.0) for g in range(KV_A)], axis=1).astype(BF16)
        o_s[...] = alpha * o_s[...] + _dot(pbd, vstack.astype(BF16))
        m_s[...] = m_new

    thr = thr_ref[...]
    kflat = jnp.concatenate([k_ref[:, g, :] for g in range(KV_A)], axis=1)
    vstack = jnp.concatenate([v_ref[:, g, :] for g in range(KV_A)], axis=0)
    update(_sortable(i_ref[...]) >= thr, kflat, vstack)

    @pl.when(p == n_pages - 1)
    def _():
        t = lax.broadcasted_iota(I32, (n_new, LANES), 0)
        s = lax.broadcasted_iota(I32, (n_new, LANES), 1)
        sel_new = (_sortable(inew_ref[...]) >= thr) & (s <= t) & (s < n_new)
        pad = jnp.zeros((LANES - n_new, KV_A * HEAD_DIM), F32)
        kn = jnp.concatenate([knew_ref[...], pad], axis=0)
        vn = jnp.concatenate([vnew_ref[...], pad], axis=0)
        vns = jnp.concatenate([vn[:, g * HEAD_DIM:(g + 1) * HEAD_DIM] for g in range(KV_A)], axis=0)
        update(sel_new, kn, vns)
        o = o_s[...] / l_s[...]
        for h in range(H_A):
            o_ref[:, h * HEAD_DIM:(h + 1) * HEAD_DIM] = o[h * n_new:(h + 1) * n_new, :]


def _dsa_sample_attend(hp, scores, thr, cache_k_a, cache_v_a, pt_flat, off, layer, mp, dec_batch, n_new, n_pages):
    kvw = KV_A * HEAD_DIM
    rb = mp // n_new
    n_rows = H_A * n_new
    page = lambda b, p, pt: (layer, pt[b * n_pages + p], 0, 0, 0)
    grid_spec = pltpu.PrefetchScalarGridSpec(
        num_scalar_prefetch=1,
        grid=(dec_batch, n_pages),
        in_specs=[pl.BlockSpec((n_new, A_W), lambda b, p, pt: (rb + b, off["q_a"] // A_W)),
                  pl.BlockSpec((None, None, n_new, LANES), lambda b, p, pt: (b, p, 0, 0)),
                  pl.BlockSpec((None, None, n_new, LANES), lambda b, p, pt: (b, n_pages, 0, 0)),
                  pl.BlockSpec((n_new, LANES), lambda b, p, pt: (b, 0)),
                  pl.BlockSpec((n_new, kvw), lambda b, p, pt: (rb + b, off["k_a"] // kvw)),
                  pl.BlockSpec((n_new, kvw), lambda b, p, pt: (rb + b, off["v_a"] // kvw)),
                  pl.BlockSpec((None, None, LANES, KV_A, HEAD_DIM), page),
                  pl.BlockSpec((None, None, LANES, KV_A, HEAD_DIM), page)],
        out_specs=pl.BlockSpec((n_new, A_W), lambda b, p, pt: (b, 0)),
        scratch_shapes=[pltpu.VMEM((n_rows, KV_A * HEAD_DIM), BF16),
                        pltpu.VMEM((n_rows, HEAD_DIM), F32),
                        pltpu.VMEM((n_rows, 1), F32),
                        pltpu.VMEM((n_rows, 1), F32)])
    return pl.pallas_call(
        functools.partial(_dsa_sample_attend_kernel, n_pages=n_pages, n_new=n_new),
        grid_spec=grid_spec,
        out_shape=jax.ShapeDtypeStruct((dec_batch * n_new, A_W), F32),
        compiler_params=_cp(("arbitrary", "arbitrary")),
        name="dsa_sample_attend",
    )(pt_flat, hp, scores, scores, thr, hp, hp, cache_k_a, cache_v_a)


def _stick_sample_kernel(pt_ref, q_ref, knew_ref, vnew_ref, k_ref, v_ref, o_ref, qbd_s, acc_s, tot_s,
                         *, n_pages, n_new):
    del pt_ref
    p = pl.program_id(1)
    n_rows = H_C * n_new
    head = lax.broadcasted_iota(I32, (n_rows, LANES), 0) // n_new
    u2 = _suffix_matrix()
    scale = HEAD_DIM ** -0.5

    def sweep(mask, kflat, vstack):
        z = _dot_nt(qbd_s[...], kflat.astype(BF16)) * scale
        lk, ls = _log_sig_pair(z)
        if mask is not None:
            lk = jnp.where(mask, lk, 0.0)
        within, total = _suffix_sums(lk, u2)
        a = jnp.exp(ls + within + tot_s[...])
        if mask is not None:
            a = jnp.where(mask, a, 0.0)
        abd = jnp.concatenate([jnp.where(head == h, a, 0.0) for h in range(H_C)], axis=1).astype(BF16)
        acc_s[...] = acc_s[...] + _dot(abd, vstack.astype(BF16))
        tot_s[...] = tot_s[...] + total

    @pl.when(p == 0)
    def _():
        qs = jnp.concatenate([q_ref[:, h * HEAD_DIM:(h + 1) * HEAD_DIM] for h in range(H_C)], axis=0)
        qbd_s[...] = jnp.concatenate([jnp.where(head == h, qs, 0.0) for h in range(H_C)], axis=1).astype(BF16)
        acc_s[...] = jnp.zeros(acc_s.shape, F32)
        tot_s[...] = jnp.zeros(tot_s.shape, F32)
        t = lax.broadcasted_iota(I32, (n_rows, LANES), 0) % n_new
        s = lax.broadcasted_iota(I32, (n_rows, LANES), 1)
        pad = jnp.zeros((LANES - n_new, C_W), F32)
        kn = jnp.concatenate([knew_ref[...], pad], axis=0)
        vn = jnp.concatenate([vnew_ref[...], pad], axis=0)
        vns = jnp.concatenate([vn[:, h * HEAD_DIM:(h + 1) * HEAD_DIM] for h in range(H_C)], axis=0)
        sweep(s < t, kn, vns)

    kflat = jnp.concatenate([k_ref[:, h, :] for h in range(H_C)], axis=1)
    vstack = jnp.concatenate([v_ref[:, h, :] for h in range(H_C)], axis=0)
    sweep(None, kflat, vstack)

    @pl.when(p == n_pages - 1)
    def _():
        acc = acc_s[...]
        for h in range(H_C):
            o_ref[:, h * HEAD_DIM:(h + 1) * HEAD_DIM] = acc[h * n_new:(h + 1) * n_new, :]


def _stick_sample(hp, cache_k_c, cache_v_c, pt_flat, off, layer, mp, dec_batch, n_new, n_pages):
    rb = mp // n_new
    n_rows = H_C * n_new
    page = lambda b, p, pt: (layer, pt[b * n_pages + n_pages - 1 - p], 0, 0, 0)
    grid_spec = pltpu.PrefetchScalarGridSpec(
        num_scalar_prefetch=1,
        grid=(dec_batch, n_pages),
        in_specs=[pl.BlockSpec((n_new, C_W), lambda b, p, pt: (rb + b, off["q_c"] // C_W)),
                  pl.BlockSpec((n_new, C_W), lambda b, p, pt: (rb + b, off["k_c"] // C_W)),
                  pl.BlockSpec((n_new, C_W), lambda b, p, pt: (rb + b, off["v_c"] // C_W)),
                  pl.BlockSpec((None, None, LANES, H_C, HEAD_DIM), page),
                  pl.BlockSpec((None, None, LANES, H_C, HEAD_DIM), page)],
        out_specs=pl.BlockSpec((n_new, C_W), lambda b, p, pt: (b, 0)),
        scratch_shapes=[pltpu.VMEM((n_rows, C_W), BF16),
                        pltpu.VMEM((n_rows, HEAD_DIM), F32),
                        pltpu.VMEM((n_rows, LANES), F32)])
    return pl.pallas_call(
        functools.partial(_stick_sample_kernel, n_pages=n_pages, n_new=n_new),
        grid_spec=grid_spec,
        out_shape=jax.ShapeDtypeStruct((dec_batch * n_new, C_W), F32),
        compiler_params=_cp(("arbitrary", "arbitrary")),
        name="stick_sample",
    )(pt_flat, hp, hp, hp, cache_k_c, cache_v_c)


def _merge_kernel(oa_ref, ob_ref, oc_ref, ga_ref, gb_ref, gc_ref, wpa_ref, wpb_ref, wpc_ref, m_ref):
    m = (jax.nn.sigmoid(ga_ref[...]) * _dot(oa_ref[...], wpa_ref[...])
         + jax.nn.sigmoid(gb_ref[...]) * _dot(ob_ref[...], wpb_ref[...])
         + jax.nn.sigmoid(gc_ref[...]) * _dot(oc_ref[...], wpc_ref[...]))
    m_ref[...] = m.astype(m_ref.dtype)


def _merge(o_a, o_b, o_c, hp, off, w_pa, w_pb, w_pc, layer):
    mt = hp.shape[0]
    d = w_pa.shape[2]
    gate = lambda name: pl.BlockSpec((TM, d), lambda i: (i, off[name] // d))
    wspec = lambda k: pl.BlockSpec((None, k, d), lambda i: (layer, 0, 0))
    return pl.pallas_call(
        _merge_kernel,
        grid=(mt // TM,),
        in_specs=[pl.BlockSpec((TM, A_W), lambda i: (i, 0)),
                  pl.BlockSpec((TM, W_B), lambda i: (i, 0)),
                  pl.BlockSpec((TM, C_W), lambda i: (i, 0)),
                  gate("g_a"), gate("g_b"), gate("g_c"),
                  wspec(A_W), wspec(W_B), wspec(C_W)],
        out_specs=pl.BlockSpec((TM, d), lambda i: (i, 0)),
        out_shape=jax.ShapeDtypeStruct((mt, d), BF16),
        compiler_params=_cp(("arbitrary",)),
        name="merge",
    )(o_a, o_b, o_c, hp, hp, hp, w_pa, w_pb, w_pc)


def _post_mix_kernel(m_ref, x_ref, wout_ref, g_ref, b_ref, wr_ref, br_ref,
                     x1_ref, x1b_ref, idx_ref, wsel_ref, *, alpha, n_valid, n_experts):
    x1 = _layer_norm(alpha * x_ref[...] + _dot(m_ref[...], wout_ref[...]), g_ref[...], b_ref[...])
    x1_ref[...] = x1
    x1b = x1.astype(BF16)
    x1b_ref[...] = x1b
    scores = jax.nn.sigmoid(_dot(x1b, wr_ref[...]))
    lane = lax.broadcasted_iota(I32, scores.shape, 1)
    lane_f = lane.astype(F32)
    sel = jnp.where(lane < n_experts, scores + br_ref[...], -jnp.inf)
    idx = jnp.full(scores.shape, -1.0, F32)
    wts = jnp.zeros(scores.shape, F32)
    for k in range(TOP_K):
        mx = jnp.max(sel, axis=1, keepdims=True)
        ik = jnp.min(jnp.where(sel == mx, lane_f, float(LANES)), axis=1, keepdims=True)
        hit = lane_f == ik
        wk = jnp.sum(jnp.where(hit, scores, 0.0), axis=1, keepdims=True)
        sel = jnp.where(hit, -jnp.inf, sel)
        idx = jnp.where(lane == k, ik, idx)
        wts = jnp.where(lane == k, wk, wts)
    wts = wts / jnp.sum(wts, axis=1, keepdims=True) * ROUTED_SCALE
    rowid = pl.program_id(0) * TM + lax.broadcasted_iota(I32, scores.shape, 0)
    valid = rowid < n_valid
    idx_ref[...] = jnp.where(valid, idx, -1.0).astype(I32)
    wsel_ref[...] = jnp.where(valid, wts, 0.0)


def _post_mix(m, xf, w_out, g, b, w_r, b_r, layer, alpha, n_valid, n_experts):
    mt, d = xf.shape
    row = lambda i: (i, 0)
    vec = lambda n: pl.BlockSpec((None, 1, n), lambda i: (layer, 0, 0))
    return pl.pallas_call(
        functools.partial(_post_mix_kernel, alpha=alpha, n_valid=n_valid, n_experts=n_experts),
        grid=(mt // TM,),
        in_specs=[pl.BlockSpec((TM, d), row), pl.BlockSpec((TM, d), row),
                  pl.BlockSpec((None, d, d), lambda i: (layer, 0, 0)),
                  vec(d), vec(d),
                  pl.BlockSpec((None, d, LANES), lambda i: (layer, 0, 0)),
                  vec(LANES)],
        out_specs=[pl.BlockSpec((TM, d), row), pl.BlockSpec((TM, d), row),
                   pl.BlockSpec((TM, LANES), row), pl.BlockSpec((TM, LANES), row)],
        out_shape=[jax.ShapeDtypeStruct((mt, d), F32), jax.ShapeDtypeStruct((mt, d), BF16),
                   jax.ShapeDtypeStruct((mt, LANES), I32), jax.ShapeDtypeStruct((mt, LANES), F32)],
        compiler_params=_cp(("arbitrary",)),
        name="post_mix",
    )(m, xf, w_out, g, b, w_r, b_r)


def _route_kernel(idx_ref, pos_ref, be_ref, nu_ref, *, mt, n_experts, nbp):
    n_tiles = mt // ROUTE_TR
    lane = lax.broadcasted_iota(I32, (ROUTE_TR, LANES), 1)

    def one_hot(t):
        idx = idx_ref[pl.ds(pl.multiple_of(t * ROUTE_TR, ROUTE_TR), ROUTE_TR), :]
        oh = jnp.zeros((ROUTE_TR, LANES), F32)
        for k in range(TOP_K):
            oh = oh + jnp.where(idx[:, k:k + 1] == lane, 1.0, 0.0)
        return idx, oh

    def count_body(t, cnt):
        return cnt + jnp.sum(one_hot(t)[1], axis=0, keepdims=True)

    cnt = lax.fori_loop(0, n_tiles, count_body, jnp.zeros((1, LANES), F32))
    nb = jnp.floor((cnt + (EXPERT_BLK - 1)) * (1.0 / EXPERT_BLK))
    r = lax.broadcasted_iota(I32, (LANES, LANES), 0)
    c = lax.broadcasted_iota(I32, (LANES, LANES), 1)
    nb8 = jnp.broadcast_to(nb, (8, LANES))
    start = _dot(nb8.astype(BF16), jnp.where(r < c, 1.0, 0.0).astype(BF16))
    pend = start + nb8
    start_row = start[0:1, :] * float(EXPERT_BLK)

    rr = lax.broadcasted_iota(I32, (ROUTE_TR, ROUTE_TR), 0)
    cc = lax.broadcasted_iota(I32, (ROUTE_TR, ROUTE_TR), 1)
    lower = jnp.where(cc < rr, 1.0, 0.0).astype(BF16)

    def pos_body(t, seen):
        idx, oh = one_hot(t)
        pe = start_row + seen + _dot(lower, oh.astype(BF16))
        pos = jnp.zeros((ROUTE_TR, LANES), F32)
        for k in range(TOP_K):
            pk = jnp.sum(jnp.where(idx[:, k:k + 1] == lane, pe, 0.0), axis=1, keepdims=True)
            pos = jnp.where(lane == k, pk, pos)
        pos_ref[pl.ds(pl.multiple_of(t * ROUTE_TR, ROUTE_TR), ROUTE_TR), :] = pos.astype(I32)
        return seen + jnp.sum(oh, axis=0, keepdims=True)

    lax.fori_loop(0, n_tiles, pos_body, jnp.zeros((1, LANES), F32))

    pend_t = jnp.broadcast_to(pend[0:1, :], (LANES, LANES)).T
    for ch in range(nbp // LANES):
        blk = (ch * LANES + c).astype(F32)
        be = jnp.sum(jnp.where((pend_t <= blk) & (r < n_experts), 1.0, 0.0), axis=0, keepdims=True)
        be = jnp.minimum(be, float(n_experts - 1))
        be_ref[:, ch * LANES:(ch + 1) * LANES] = jnp.broadcast_to(be, (8, LANES)).astype(I32)
    nu_ref[...] = jnp.broadcast_to(jnp.max(pend, axis=1, keepdims=True), (8, LANES)).astype(I32)


def _route(idx, n_experts, n_blocks):
    mt = idx.shape[0]
    nbp = -(-n_blocks // LANES) * LANES
    return pl.pallas_call(
        functools.partial(_route_kernel, mt=mt, n_experts=n_experts, nbp=nbp),
        out_shape=[jax.ShapeDtypeStruct((mt, LANES), I32),
                   jax.ShapeDtypeStruct((8, nbp), I32),
                   jax.ShapeDtypeStruct((8, LANES), I32)],
        compiler_params=pltpu.CompilerParams(vmem_limit_bytes=VMEM_LIMIT),
        name="route",
    )(idx)


def _dispatch_kernel(pos_ref, x_ref, xs_in_ref, xs_ref, sem, *, n_valid):
    del xs_in_ref
    t0 = pl.program_id(0) * DISPATCH_TD
    n = jnp.clip(n_valid - t0, 0, DISPATCH_TD)

    def copy(r, k):
        dst = pos_ref[(t0 + r) * TOP_K + k]
        return pltpu.make_async_copy(x_ref.at[pl.ds(r, 1), :], xs_ref.at[pl.ds(dst, 1), :], sem)

    def start_body(r, c):
        for k in range(TOP_K):
            copy(r, k).start()
        return c

    def wait_body(r, c):
        for k in range(TOP_K):
            copy(r, k).wait()
        return c

    lax.fori_loop(0, n, start_body, 0)
    lax.fori_loop(0, n, wait_body, 0)


def _dispatch(pos_flat, x1, n_rows, n_valid):
    mt, d = x1.shape
    grid_spec = pltpu.PrefetchScalarGridSpec(
        num_scalar_prefetch=1,
        grid=(mt // DISPATCH_TD,),
        in_specs=[pl.BlockSpec((DISPATCH_TD, d), lambda i, pos: (i, 0)),
                  pl.BlockSpec(memory_space=pl.ANY)],
        out_specs=pl.BlockSpec(memory_space=pl.ANY),
        scratch_shapes=[pltpu.SemaphoreType.DMA(())])
    return pl.pallas_call(
        functools.partial(_dispatch_kernel, n_valid=n_valid),
        grid_spec=grid_spec,
        out_shape=jax.ShapeDtypeStruct((n_rows, d), F32),
        input_output_aliases={2: 0},
        compiler_params=_cp(("arbitrary",)),
        name="dispatch",
    )(pos_flat, x1, jnp.zeros((n_rows, d), F32))


def _expert_kernel(be_ref, nu_ref, x_ref, wg_ref, wu_ref, wd_ref, y_ref):
    del be_ref
    used = pl.program_id(0) < nu_ref[0]

    @pl.when(used)
    def _():
        x = x_ref[...].astype(BF16)
        a = jax.nn.silu(_dot(x, wg_ref[...])) * _dot(x, wu_ref[...])
        y_ref[...] = _dot(a.astype(BF16), wd_ref[...])

    @pl.when(jnp.logical_not(used))
    def _():
        y_ref[...] = jnp.zeros(y_ref.shape, F32)


def _experts(xs, be, nu, w_gate, w_up, w_down, layer, n_blocks):
    n_rows, d = xs.shape
    f = w_gate.shape[3]
    blk = lambda i, be, nu: (jnp.minimum(i, nu[0] - 1), 0)
    grid_spec = pltpu.PrefetchScalarGridSpec(
        num_scalar_prefetch=2,
        grid=(n_blocks,),
        in_specs=[pl.BlockSpec((EXPERT_BLK, d), blk),
                  pl.BlockSpec((None, None, d, f), lambda i, be, nu: (layer, be[i], 0, 0)),
                  pl.BlockSpec((None, None, d, f), lambda i, be, nu: (layer, be[i], 0, 0)),
                  pl.BlockSpec((None, None, f, d), lambda i, be, nu: (layer, be[i], 0, 0))],
        out_specs=pl.BlockSpec((EXPERT_BLK, d), lambda i, be, nu: (i, 0)))
    return pl.pallas_call(
        _expert_kernel,
        grid_spec=grid_spec,
        out_shape=jax.ShapeDtypeStruct((n_rows, d), F32),
        compiler_params=_cp(("arbitrary",)),
        name="experts",
    )(be, nu, xs, w_gate, w_up, w_down)


def _combine_kernel(pos_ref, ys_ref, wsel_ref, x1_ref, x1b_ref, wsg_ref, wsu_ref, wsd_ref, g_ref, b_ref,
                    x2_ref, x2b_ref, buf, sem, *, alpha):
    t0 = pl.program_id(0) * COMBINE_TC

    def copy(r, k):
        src = pos_ref[(t0 + r) * TOP_K + k]
        return pltpu.make_async_copy(ys_ref.at[pl.ds(src, 1), :], buf.at[k, pl.ds(r, 1), :], sem)

    def start_body(r, c):
        for k in range(TOP_K):
            copy(r, k).start()
        return c

    def wait_body(r, c):
        for k in range(TOP_K):
            copy(r, k).wait()
        return c

    lax.fori_loop(0, COMBINE_TC, start_body, 0)
    x1b = x1b_ref[...]
    shared = _dot((jax.nn.silu(_dot(x1b, wsg_ref[...])) * _dot(x1b, wsu_ref[...])).astype(BF16), wsd_ref[...])
    lax.fori_loop(0, COMBINE_TC, wait_body, 0)
    wsel = wsel_ref[...]
    routed = wsel[:, 0:1] * buf[0]
    for k in range(1, TOP_K):
        routed = routed + wsel[:, k:k + 1] * buf[k]
    x2 = _layer_norm(alpha * x1_ref[...] + (routed + shared), g_ref[...], b_ref[...])
    x2_ref[...] = x2
    x2b_ref[...] = x2.astype(BF16)


def _combine(pos_flat, ys, wsel, x1, x1b, ws_gate, ws_up, ws_down, g, b, layer, alpha):
    mt, d = x1.shape
    f = ws_gate.shape[2]
    row = lambda i, pos: (i, 0)
    vec = pl.BlockSpec((None, 1, d), lambda i, pos: (layer, 0, 0))
    grid_spec = pltpu.PrefetchScalarGridSpec(
        num_scalar_prefetch=1,
        grid=(mt // COMBINE_TC,),
        in_specs=[pl.BlockSpec(memory_space=pl.ANY),
                  pl.BlockSpec((COMBINE_TC, LANES), row),
                  pl.BlockSpec((COMBINE_TC, d), row),
                  pl.BlockSpec((COMBINE_TC, d), row),
                  pl.BlockSpec((None, d, f), lambda i, pos: (layer, 0, 0)),
                  pl.BlockSpec((None, d, f), lambda i, pos: (layer, 0, 0)),
                  pl.BlockSpec((None, f, d), lambda i, pos: (layer, 0, 0)),
                  vec, vec],
        out_specs=[pl.BlockSpec((COMBINE_TC, d), row), pl.BlockSpec((COMBINE_TC, d), row)],
        scratch_shapes=[pltpu.VMEM((TOP_K, COMBINE_TC, d), F32), pltpu.SemaphoreType.DMA(())])
    return pl.pallas_call(
        functools.partial(_combine_kernel, alpha=alpha),
        grid_spec=grid_spec,
        out_shape=[jax.ShapeDtypeStruct((mt, d), F32), jax.ShapeDtypeStruct((mt, d), BF16)],
        compiler_params=_cp(("arbitrary",)),
        name="combine",
    )(pos_flat, ys, wsel, x1, x1b, ws_gate, ws_up, ws_down, g, b)


def kernel(x_prompt, x_sample, cache_k_a, cache_v_a, cache_idx_k, cache_k_c, cache_v_c, page_table, w_in, w_s, b_s, ln_v_g, ln_v_b, w_pa, w_pb, w_pc, w_out, ln1_g, ln1_b, w_router, b_router, w_gate, w_up, w_down, ws_gate, ws_up, ws_down, ln2_g, ln2_b):
    batch, seq, d = x_prompt.shape
    dec_batch, n_new, _ = x_sample.shape
    depth = w_in.shape[0]
    n_pages = page_table.shape[1]
    n_experts = w_router.shape[2]
    alpha = float((2 * depth) ** 0.25)
    mp, ms = batch * seq, dec_batch * n_new
    n_valid = mp + ms
    mt = -(-n_valid // TM) * TM
    assert seq % QB == 0 and n_new == 8 and n_experts <= LANES and mp % TM == 0
    n_blocks = -(-(n_valid * TOP_K + n_experts * (EXPERT_BLK - 1)) // EXPERT_BLK)
    n_rows = n_blocks * EXPERT_BLK

    pieces, src, off, n_cols = _layout(d)
    cols = []
    for name, w in pieces:
        s0, sw = src[name]
        piece = w_in[:, :, s0:s0 + sw]
        if sw < w:
            piece = jnp.pad(piece, ((0, 0), (0, 0), (0, w - sw)))
        cols.append(piece)
    w_in_r = jnp.concatenate(cols, axis=2).astype(BF16)

    bf = lambda a: a.astype(BF16)
    w_pa_b, w_pb_b, w_pc_b, w_out_b = bf(w_pa), bf(w_pb), bf(w_pc), bf(w_out)
    w_gate_b, w_up_b, w_down_b = bf(w_gate), bf(w_up), bf(w_down)
    ws_gate_b, ws_up_b, ws_down_b = bf(ws_gate), bf(ws_up), bf(ws_down)
    w_r_b = bf(jnp.pad(w_router, ((0, 0), (0, 0), (0, LANES - n_experts))))
    b_r = jnp.pad(b_router, ((0, 0), (0, LANES - n_experts)))[:, None, :]
    b_s_t = jnp.swapaxes(b_s, 1, 2)
    vec3 = lambda a: a[:, None, :]
    ln1_g3, ln1_b3, ln2_g3, ln2_b3 = vec3(ln1_g), vec3(ln1_b), vec3(ln2_g), vec3(ln2_b)
    pt_flat = page_table.reshape(-1).astype(I32)

    xf = jnp.concatenate([x_prompt.reshape(mp, d), x_sample.reshape(ms, d),
                          jnp.zeros((mt - n_valid, d), F32)], axis=0)
    xb = xf.astype(BF16)

    st = [[] for _ in range(11)]
    for l in range(depth):
        hp = _in_proj(xb, w_in_r, l)

        def cut(name, lo, hi, width):
            return hp[lo:hi, off[name]:off[name] + width]

        o_a_p = _dsa_prompt(hp, off, batch, seq)
        o_c_p = _stick_prompt(hp, off, batch, seq)
        o_b_p, _ = _gmlp(hp, hp, off["u_b"] // W_B, off["v_b"] // W_B, mp // CHUNK,
                         w_s[l], b_s_t[l], ln_v_g[l][None], ln_v_b[l][None])

        iw_s = cut("iw", mp, mp + ms, IDX_H).reshape(dec_batch, n_new, IDX_H)
        wrow = jnp.swapaxes(iw_s, 1, 2).reshape(dec_batch, 1, IDX_H * n_new)
        wrow = jnp.broadcast_to(wrow, (dec_batch, n_new, IDX_H * n_new)).reshape(ms, IDX_H * n_new)
        scores, thr = _dsa_sample_scores(hp, wrow, cache_idx_k, pt_flat, off, l, mp, dec_batch, n_new, n_pages)
        o_a_s = _dsa_sample_attend(hp, scores, thr, cache_k_a, cache_v_a, pt_flat, off, l, mp,
                                   dec_batch, n_new, n_pages)
        o_c_s = _stick_sample(hp, cache_k_c, cache_v_c, pt_flat, off, l, mp, dec_batch, n_new, n_pages)
        pad_chunk = lambda a: jnp.pad(a.reshape(dec_batch, n_new, W_B),
                                      ((0, 0), (0, CHUNK - n_new), (0, 0))).reshape(dec_batch * CHUNK, W_B)
        o_b_s, vn_s = _gmlp(pad_chunk(cut("u_b", mp, mp + ms, W_B)), pad_chunk(cut("v_b", mp, mp + ms, W_B)),
                            0, 0, dec_batch, w_s[l], b_s_t[l], ln_v_g[l][None], ln_v_b[l][None])
        unpad = lambda a: a.reshape(dec_batch, CHUNK, W_B)[:, :n_new].reshape(ms, W_B)

        tail = lambda w, dt: jnp.zeros((mt - n_valid, w), dt)
        o_a = jnp.concatenate([o_a_p, o_a_s.astype(BF16), tail(A_W, BF16)], axis=0)
        o_b = jnp.concatenate([o_b_p, unpad(o_b_s), tail(W_B, BF16)], axis=0)
        o_c = jnp.concatenate([o_c_p, o_c_s.astype(BF16), tail(C_W, BF16)], axis=0)

        m = _merge(o_a, o_b, o_c, hp, off, w_pa_b, w_pb_b, w_pc_b, l)
        x1, x1b, idx, wsel = _post_mix(m, xf, w_out_b, ln1_g3, ln1_b3, w_r_b, b_r, l, alpha, n_valid, n_experts)

        pos, be, nu = _route(idx, n_experts, n_blocks)
        pos_flat = pos[:, :TOP_K].reshape(-1)
        xs = _dispatch(pos_flat, x1, n_rows, n_valid)
        ys = _experts(xs, be[0, :n_blocks], nu[0, :1], w_gate_b, w_up_b, w_down_b, l, n_blocks)
        xf, xb = _combine(pos_flat, ys, wsel, x1, x1b, ws_gate_b, ws_up_b, ws_down_b, ln2_g3, ln2_b3, l, alpha)

        kvw = KV_A * HEAD_DIM
        for lst, (name, w, shp) in zip(st[:5], (("k_a", kvw, (KV_A, HEAD_DIM)), ("v_a", kvw, (KV_A, HEAD_DIM)),
                                                ("ik", IDX_DIM, (IDX_DIM,)), ("k_c", C_W, (H_C, HEAD_DIM)),
                                                ("v_c", C_W, (H_C, HEAD_DIM)))):
            lst.append(cut(name, 0, mp, w).reshape(batch, seq, *shp))
        for lst, (name, w, shp) in zip(st[5:10], (("k_a", kvw, (KV_A, HEAD_DIM)), ("v_a", kvw, (KV_A, HEAD_DIM)),
                                                  ("ik", IDX_DIM, (IDX_DIM,)), ("k_c", C_W, (H_C, HEAD_DIM)),
                                                  ("v_c", C_W, (H_C, HEAD_DIM)))):
            lst.append(cut(name, mp, mp + ms, w).reshape(dec_batch, n_new, *shp))
        st[10].append(unpad(vn_s).reshape(dec_batch, n_new, W_B))

    y_prompt = xf[:mp].reshape(batch, seq, d)
    y_sample = xf[mp:mp + ms].reshape(dec_batch, n_new, d)
    return (y_prompt, y_sample) + tuple(jnp.stack(s) for s in st)
```

```python
import functools

import jax
import jax.numpy as jnp
from jax import lax
from jax.experimental import pallas as pl
from jax.experimental.pallas import tpu as pltpu

F32 = jnp.float32
BF16 = jnp.bfloat16
I32 = jnp.int32

HEAD_DIM = 128
H_A = 6
KV_A = 2
IDX_H = 16
IDX_DIM = 128
TOPK_MAX = 256
CHUNK = 128
G_B = 4
W_B = 512
H_C = 6
QB = 128
TOP_K = 6
ROUTED_SCALE = 2.5
LN_EPS = 1e-5
A_W = H_A * HEAD_DIM
C_W = H_C * HEAD_DIM
LANES = 128
TM = 512
EXPERT_BLK = 256
ROUTE_TR = 256
DISPATCH_TD = 256
COMBINE_TC = 128
DSA_KT = 256
STICK_HG = 3
STICK_EXIT = -110.0
PAGE_GROUP = 8
NEG = -1e30
INT_MIN = -2147483648
VMEM_LIMIT = 56 * 1024 * 1024


def _cp(sem, vmem=VMEM_LIMIT):
    return pltpu.CompilerParams(dimension_semantics=sem, vmem_limit_bytes=vmem)


def _layout(d_model):
    pieces = (("iq", IDX_H * IDX_DIM), ("g_a", d_model), ("g_b", d_model), ("g_c", d_model),
              ("k_a", KV_A * HEAD_DIM), ("q_a", A_W), ("q_c", C_W), ("k_c", C_W), ("v_c", C_W),
              ("v_a", KV_A * HEAD_DIM), ("u_b", W_B), ("v_b", W_B), ("ik", IDX_DIM), ("iw", LANES))
    src_order = (("q_a", A_W), ("k_a", KV_A * HEAD_DIM), ("v_a", KV_A * HEAD_DIM), ("iq", IDX_H * IDX_DIM),
                 ("ik", IDX_DIM), ("iw", IDX_H), ("u_b", W_B), ("v_b", W_B), ("q_c", C_W), ("k_c", C_W),
                 ("v_c", C_W), ("g_a", d_model), ("g_b", d_model), ("g_c", d_model))
    src, acc = {}, 0
    for name, w in src_order:
        src[name] = (acc, w)
        acc += w
    off, acc = {}, 0
    for name, w in pieces:
        assert acc % w == 0, (name, acc, w)
        off[name] = acc
        acc += w
    return pieces, src, off, acc


def _sortable(x):
    bits = lax.bitcast_convert_type(x, I32)
    return jnp.where(bits < 0, bits ^ 0x7FFFFFFF, bits)


def _gelu(x):
    return 0.5 * x * (1.0 + lax.erf(x * (2.0 ** -0.5)))


def _layer_norm(y, g, b):
    mu = jnp.mean(y, axis=-1, keepdims=True)
    yc = y - mu
    var = jnp.mean(yc * yc, axis=-1, keepdims=True)
    return yc * lax.rsqrt(var + LN_EPS) * g + b


def _dot(a, b):
    return jnp.dot(a, b, preferred_element_type=F32)


def _dot_nt(a, b):
    return lax.dot_general(a, b, (((1,), (1,)), ((), ())), preferred_element_type=F32)


def _log_sig_pair(z):
    sp = jnp.log1p(jnp.exp(-jnp.abs(z)))
    return -(jnp.maximum(z, 0.0) + sp), -(jnp.maximum(-z, 0.0) + sp)


def _suffix_matrix():
    r = lax.broadcasted_iota(I32, (LANES, 2 * LANES), 0)
    c = lax.broadcasted_iota(I32, (LANES, 2 * LANES), 1)
    return jnp.where((c >= LANES) | (r > c), 1.0, 0.0).astype(BF16)


def _suffix_sums(lk, u2):
    hi = lk.astype(BF16)
    lo = (lk - hi.astype(F32)).astype(BF16)
    cs = _dot(hi, u2) + _dot(lo, u2)
    return cs[:, :LANES], cs[:, LANES:]


def _matmul_kernel(x_ref, w_ref, o_ref):
    o_ref[...] = _dot(x_ref[...], w_ref[...])


def _in_proj(xb, w_r, layer):
    mt, d = xb.shape
    n = w_r.shape[2]
    tn = max(t for t in range(LANES, 1025, LANES) if n % t == 0)
    return pl.pallas_call(
        _matmul_kernel,
        grid=(n // tn, mt // TM),
        in_specs=[pl.BlockSpec((TM, d), lambda j, i: (i, 0)),
                  pl.BlockSpec((None, d, tn), lambda j, i: (layer, 0, j))],
        out_specs=pl.BlockSpec((TM, tn), lambda j, i: (i, j)),
        out_shape=jax.ShapeDtypeStruct((mt, n), F32),
        compiler_params=_cp(("arbitrary", "arbitrary")),
        name="in_proj",
    )(xb, w_r)


def _dsa_prompt_kernel(q_ref, iq_ref, iw_ref, k_ref, v_ref, ik_ref, o_ref,
                       iqt_s, key_s, vt_s, qb_s, ot_s, *, seq, topk):
    i = pl.program_id(1)
    n_t = (i * QB + QB + DSA_KT - 1) // DSA_KT

    def key_rows(j):
        return pl.ds(pl.multiple_of(j * DSA_KT, DSA_KT), DSA_KT)

    @pl.when(i == 0)
    def _():
        def body(j, c):
            for g in range(KV_A):
                vt_s[g, j] = v_ref[key_rows(j), g * HEAD_DIM:(g + 1) * HEAD_DIM].T.astype(BF16)
            return c
        lax.fori_loop(0, seq // DSA_KT, body, 0)

    for h in range(IDX_H):
        iqt_s[:, h * QB:(h + 1) * QB] = iq_ref[:, h * IDX_DIM:(h + 1) * IDX_DIM].T.astype(BF16)
    qb_s[...] = q_ref[...].astype(BF16)
    iwt = iw_ref[...].T * ((IDX_DIM ** -0.5) * (IDX_H ** -0.5))

    row = lax.broadcasted_iota(I32, (DSA_KT, QB), 0)
    col = lax.broadcasted_iota(I32, (DSA_KT, QB), 1)

    def causal(j):
        return (j * DSA_KT + row) <= (i * QB + col)

    def score_body(j, c):
        ikt = ik_ref[key_rows(j), :].astype(BF16)
        s_all = _dot(ikt, iqt_s[...])
        acc = jnp.zeros((DSA_KT, QB), F32)
        for h in range(IDX_H):
            acc = acc + jnp.maximum(s_all[:, h * QB:(h + 1) * QB], 0.0) * iwt[h:h + 1, :]
        key_s[j] = _sortable(jnp.where(causal(j), acc, -jnp.inf))
        return c

    lax.fori_loop(0, n_t, score_body, 0)

    def bit_body(bi, tu):
        candu = tu | lax.shift_left(jnp.int32(1), 31 - bi)
        cand = candu ^ INT_MIN

        def count_body(j, cnt):
            for half in range(DSA_KT // QB):
                cnt = cnt + jnp.where(key_s[j, half * QB:(half + 1) * QB, :] >= cand, 1.0, 0.0)
            return cnt

        cnt = lax.fori_loop(0, n_t, count_body, jnp.zeros((QB, QB), F32))
        return jnp.where(jnp.sum(cnt, axis=0, keepdims=True) >= topk, candu, tu)

    thr = lax.fori_loop(0, 32, bit_body, jnp.zeros((1, QB), I32)) ^ INT_MIN

    ot_s[...] = jnp.zeros(ot_s.shape, F32)
    scale = HEAD_DIM ** -0.5
    rep = H_A // KV_A

    def logits(j):
        sel = (key_s[j] >= thr) & causal(j)
        out = []
        for g in range(KV_A):
            kt = k_ref[key_rows(j), g * HEAD_DIM:(g + 1) * HEAD_DIM].astype(BF16)
            for r in range(rep):
                h = g * rep + r
                out.append(_dot_nt(kt, qb_s[:, h * HEAD_DIM:(h + 1) * HEAD_DIM]) * scale)
        return sel, out

    def max_body(j, ms):
        sel, sts = logits(j)
        return tuple(jnp.maximum(m, jnp.max(jnp.where(sel, st, NEG), axis=0, keepdims=True))
                     for m, st in zip(ms, sts))

    ms = lax.fori_loop(0, n_t, max_body, tuple(jnp.full((1, QB), NEG, F32) for _ in range(H_A)))

    def att_body(j, ls):
        sel, sts = logits(j)
        out = []
        for h in range(H_A):
            p = jnp.where(sel, jnp.exp(sts[h] - ms[h]), 0.0)
            out.append(ls[h] + jnp.sum(p, axis=0, keepdims=True))
            ot_s[h] += _dot(vt_s[h // rep, j], p.astype(BF16))
        return tuple(out)

    ls = lax.fori_loop(0, n_t, att_body, tuple(jnp.zeros((1, QB), F32) for _ in range(H_A)))

    for h in range(H_A):
        o_ref[:, h * HEAD_DIM:(h + 1) * HEAD_DIM] = (ot_s[h] / ls[h]).T.astype(o_ref.dtype)


def _dsa_prompt(hp, off, batch, seq):
    nq = seq // QB
    topk = min(TOPK_MAX, seq // 4)
    kvw = KV_A * HEAD_DIM
    iqw = IDX_H * IDX_DIM
    return pl.pallas_call(
        functools.partial(_dsa_prompt_kernel, seq=seq, topk=topk),
        grid=(batch, nq),
        in_specs=[pl.BlockSpec((QB, A_W), lambda b, i: (b * nq + i, off["q_a"] // A_W)),
                  pl.BlockSpec((QB, iqw), lambda b, i: (b * nq + i, off["iq"] // iqw)),
                  pl.BlockSpec((QB, LANES), lambda b, i: (b * nq + i, off["iw"] // LANES)),
                  pl.BlockSpec((seq, kvw), lambda b, i: (b, off["k_a"] // kvw)),
                  pl.BlockSpec((seq, kvw), lambda b, i: (b, off["v_a"] // kvw)),
                  pl.BlockSpec((seq, IDX_DIM), lambda b, i: (b, off["ik"] // IDX_DIM))],
        out_specs=pl.BlockSpec((QB, A_W), lambda b, i: (b * nq + i, 0)),
        out_shape=jax.ShapeDtypeStruct((batch * seq, A_W), BF16),
        scratch_shapes=[pltpu.VMEM((IDX_DIM, IDX_H * QB), BF16),
                        pltpu.VMEM((seq // DSA_KT, DSA_KT, QB), I32),
                        pltpu.VMEM((KV_A, seq // DSA_KT, HEAD_DIM, DSA_KT), BF16),
                        pltpu.VMEM((QB, A_W), BF16),
                        pltpu.VMEM((H_A, HEAD_DIM, QB), F32)],
        compiler_params=_cp(("arbitrary", "arbitrary")),
        name="dsa_prompt",
    )(hp, hp, hp, hp, hp, hp)


def _stick_prompt_kernel(q_ref, k_ref, v_ref, o_ref):
    i = pl.program_id(2)
    u2 = _suffix_matrix()
    row = lax.broadcasted_iota(I32, (QB, QB), 0)
    col = lax.broadcasted_iota(I32, (QB, QB), 1)
    scale = HEAD_DIM ** -0.5
    heads = [slice(h * HEAD_DIM, (h + 1) * HEAD_DIM) for h in range(STICK_HG)]
    qs = [q_ref[:, hs].astype(BF16) for hs in heads]

    def cond(c):
        return (c[0] <= i) & (c[1] == 0)

    def body(c):
        jj, _, accs, tots = c
        j = i - jj
        rows = pl.ds(pl.multiple_of(j * QB, QB), QB)
        mask = (j * QB + col) < (i * QB + row)
        new_accs, new_tots = [], []
        for q, hs, acc, tot in zip(qs, heads, accs, tots):
            z = _dot_nt(q, k_ref[rows, hs].astype(BF16)) * scale
            lk, ls = _log_sig_pair(z)
            lk = jnp.where(mask, lk, 0.0)
            within, total = _suffix_sums(lk, u2)
            a = jnp.where(mask, jnp.exp(ls + within + tot), 0.0)
            new_accs.append(acc + _dot(a.astype(BF16), v_ref[rows, hs].astype(BF16)))
            new_tots.append(tot + total)
        worst = functools.reduce(jnp.maximum, new_tots)
        done = (jnp.max(worst) < STICK_EXIT).astype(I32)
        return jj + 1, done, tuple(new_accs), tuple(new_tots)

    zeros = tuple(jnp.zeros((QB, QB), F32) for _ in range(STICK_HG))
    _, _, accs, _ = lax.while_loop(cond, body, (jnp.int32(0), jnp.int32(0), zeros, zeros))
    for hs, acc in zip(heads, accs):
        o_ref[:, hs] = acc.astype(o_ref.dtype)


def _stick_prompt(hp, off, batch, seq):
    nq = seq // QB
    w = STICK_HG * HEAD_DIM
    assert all(off[n] % w == 0 for n in ("q_c", "k_c", "v_c"))
    return pl.pallas_call(
        _stick_prompt_kernel,
        grid=(batch, H_C // STICK_HG, nq),
        in_specs=[pl.BlockSpec((QB, w), lambda b, h, i: (b * nq + i, off["q_c"] // w + h)),
                  pl.BlockSpec((seq, w), lambda b, h, i: (b, off["k_c"] // w + h)),
                  pl.BlockSpec((seq, w), lambda b, h, i: (b, off["v_c"] // w + h))],
        out_specs=pl.BlockSpec((QB, w), lambda b, h, i: (b * nq + i, h)),
        out_shape=jax.ShapeDtypeStruct((batch * seq, C_W), BF16),
        compiler_params=_cp(("arbitrary", "arbitrary", "arbitrary")),
        name="stick_prompt",
    )(hp, hp, hp)


def _gmlp_kernel(u_ref, v_ref, ws_ref, bst_ref, g_ref, b_ref, o_ref, vn_ref):
    vn = _layer_norm(_gelu(v_ref[...]), g_ref[...], b_ref[...])
    vn_ref[...] = vn
    gu = _gelu(u_ref[...])
    row = lax.broadcasted_iota(I32, (CHUNK, CHUNK), 0)
    col = lax.broadcasted_iota(I32, (CHUNK, CHUNK), 1)
    gw = W_B // G_B
    for g in range(G_B):
        w = jnp.where(row >= col, ws_ref[g], 0.0).astype(BF16)
        mixed = _dot(w, vn[:, g * gw:(g + 1) * gw].astype(BF16)) + bst_ref[:, g:g + 1]
        o_ref[:, g * gw:(g + 1) * gw] = (gu[:, g * gw:(g + 1) * gw] * mixed).astype(o_ref.dtype)


def _gmlp(u_src, v_src, u_blk, v_blk, n_chunks, w_s, b_s_t, g, b):
    return pl.pallas_call(
        _gmlp_kernel,
        grid=(n_chunks,),
        in_specs=[pl.BlockSpec((CHUNK, W_B), lambda c: (c, u_blk)),
                  pl.BlockSpec((CHUNK, W_B), lambda c: (c, v_blk)),
                  pl.BlockSpec((G_B, CHUNK, CHUNK), lambda c: (0, 0, 0)),
                  pl.BlockSpec((CHUNK, G_B), lambda c: (0, 0)),
                  pl.BlockSpec((1, W_B), lambda c: (0, 0)),
                  pl.BlockSpec((1, W_B), lambda c: (0, 0))],
        out_specs=[pl.BlockSpec((CHUNK, W_B), lambda c: (c, 0)),
                   pl.BlockSpec((CHUNK, W_B), lambda c: (c, 0))],
        out_shape=[jax.ShapeDtypeStruct((n_chunks * CHUNK, W_B), BF16),
                   jax.ShapeDtypeStruct((n_chunks * CHUNK, W_B), F32)],
        compiler_params=_cp(("arbitrary",)),
        name="gmlp",
    )(u_src, v_src, w_s, b_s_t, g, b)


def _page_specs(block, n_pages, layer, reverse=False):
    def spec(g):
        def index(b, p, pt):
            k = p * PAGE_GROUP + g
            k = n_pages - 1 - k if reverse else k
            return (layer, pt[b * n_pages + k]) + (0,) * (len(block) - 2)
        return pl.BlockSpec(block, index)
    return [spec(g) for g in range(PAGE_GROUP)]


def _dsa_sample_score_kernel(pt_ref, iq_ref, wrow_ref, iknew_ref, *rest, n_groups, n_new, topk):
    del pt_ref
    ik_refs = rest[:PAGE_GROUP]
    i_ref, thr_ref, q2_s, key_s = rest[PAGE_GROUP:]
    p = pl.program_id(1)
    gw = PAGE_GROUP * LANES

    @pl.when(p == 0)
    def _():
        q2 = jnp.concatenate([iq_ref[:, h * IDX_DIM:(h + 1) * IDX_DIM] for h in range(IDX_H)], axis=0)
        q2_s[...] = q2.astype(BF16)

    wrow = wrow_ref[0:1, :] * ((IDX_DIM ** -0.5) * (IDX_H ** -0.5))
    r16 = lax.broadcasted_iota(I32, (2 * n_new, LANES), 0)
    l16 = lax.broadcasted_iota(I32, (2 * n_new, LANES), 1)
    head_sum = jnp.where(l16 % n_new == r16, 1.0, 0.0).astype(BF16)

    def scores(keys):
        r = jnp.maximum(_dot_nt(keys.astype(BF16), q2_s[...]), 0.0) * wrow
        hi = r.astype(BF16)
        lo = (r - hi.astype(F32)).astype(BF16)
        return (_dot_nt(head_sum, hi) + _dot_nt(head_sum, lo))[0:n_new, :]

    it = scores(jnp.concatenate([ref[...] for ref in ik_refs], axis=0))
    i_ref[p] = it
    key_s[p] = _sortable(it)

    @pl.when(p == n_groups - 1)
    def _():
        pad = jnp.zeros((LANES - n_new, IDX_DIM), F32)
        itn = scores(jnp.concatenate([iknew_ref[...], pad], axis=0))
        t = lax.broadcasted_iota(I32, (n_new, LANES), 0)
        s = lax.broadcasted_iota(I32, (n_new, LANES), 1)
        itn = jnp.where((s <= t) & (s < n_new), itn, -jnp.inf)
        itn = jnp.concatenate([itn, jnp.full((n_new, gw - LANES), -jnp.inf, F32)], axis=1)
        i_ref[n_groups] = itn
        key_s[n_groups] = _sortable(itn)

        def bit_body(bi, tu):
            candu = tu | lax.shift_left(jnp.int32(1), 31 - bi)
            cand = candu ^ INT_MIN

            def count_body(j, cnt):
                return cnt + jnp.where(key_s[j] >= cand, 1.0, 0.0)

            cnt = lax.fori_loop(0, n_groups + 1, count_body, jnp.zeros((n_new, gw), F32))
            return jnp.where(jnp.sum(cnt, axis=1, keepdims=True) >= topk, candu, tu)

        thr = lax.fori_loop(0, 32, bit_body, jnp.zeros((n_new, 1), I32)) ^ INT_MIN
        thr_ref[...] = jnp.broadcast_to(thr, thr_ref.shape)


def _dsa_sample_scores(hp, wrow, cache_idx_k, pt_flat, off, layer, mp, dec_batch, n_new, n_pages):
    topk = min(TOPK_MAX, (n_pages * LANES + n_new) // 4)
    iqw = IDX_H * IDX_DIM
    rb = mp // n_new
    n_groups = n_pages // PAGE_GROUP
    gw = PAGE_GROUP * LANES
    grid_spec = pltpu.PrefetchScalarGridSpec(
        num_scalar_prefetch=1,
        grid=(dec_batch, n_groups),
        in_specs=[pl.BlockSpec((n_new, iqw), lambda b, p, pt: (rb + b, off["iq"] // iqw)),
                  pl.BlockSpec((n_new, LANES), lambda b, p, pt: (b, 0)),
                  pl.BlockSpec((n_new, IDX_DIM), lambda b, p, pt: (rb + b, off["ik"] // IDX_DIM))]
        + _page_specs((None, None, LANES, IDX_DIM), n_pages, layer),
        out_specs=[pl.BlockSpec((None, n_groups + 1, n_new, gw), lambda b, p, pt: (b, 0, 0, 0)),
                   pl.BlockSpec((n_new, LANES), lambda b, p, pt: (b, 0))],
        scratch_shapes=[pltpu.VMEM((IDX_H * n_new, IDX_DIM), BF16),
                        pltpu.VMEM((n_groups + 1, n_new, gw), I32)])
    return pl.pallas_call(
        functools.partial(_dsa_sample_score_kernel, n_groups=n_groups, n_new=n_new, topk=topk),
        grid_spec=grid_spec,
        out_shape=[jax.ShapeDtypeStruct((dec_batch, n_groups + 1, n_new, gw), F32),
                   jax.ShapeDtypeStruct((dec_batch * n_new, LANES), I32)],
        compiler_params=_cp(("arbitrary", "arbitrary")),
        name="dsa_sample_scores",
    )(pt_flat, hp, wrow, hp, *([cache_idx_k] * PAGE_GROUP))


def _dsa_sample_attend_kernel(pt_ref, q_ref, i_ref, inew_ref, thr_ref, knew_ref, vnew_ref, *rest, n_groups, n_new):
    del pt_ref
    k_refs, v_refs = rest[:PAGE_GROUP], rest[PAGE_GROUP:2 * PAGE_GROUP]
    o_ref, qbd_s, o_s, m_s, l_s = rest[2 * PAGE_GROUP:]
    p = pl.program_id(1)
    rep = H_A // KV_A
    n_rows = H_A * n_new
    grp = lax.broadcasted_iota(I32, (n_rows, 1), 0) // (rep * n_new)
    scale = HEAD_DIM ** -0.5

    @pl.when(p == 0)
    def _():
        qs = jnp.concatenate([q_ref[:, h * HEAD_DIM:(h + 1) * HEAD_DIM] for h in range(H_A)], axis=0)
        qbd_s[...] = jnp.concatenate([jnp.where(grp == g, qs, 0.0) for g in range(KV_A)], axis=1).astype(BF16)
        m_s[...] = jnp.full(m_s.shape, NEG, F32)
        l_s[...] = jnp.zeros(l_s.shape, F32)
        o_s[...] = jnp.zeros(o_s.shape, F32)

    def update(sel_q, kflat, vstack):
        sel = jnp.concatenate([jnp.where(sel_q, 1.0, 0.0)] * H_A, axis=0) > 0.5
        s = _dot_nt(qbd_s[...], kflat.astype(BF16)) * scale
        m_old = m_s[...]
        m_new = jnp.maximum(m_old, jnp.max(jnp.where(sel, s, NEG), axis=1, keepdims=True))
        pr = jnp.where(sel, jnp.exp(s - m_new), 0.0)
        alpha = jnp.exp(m_old - m_new)
        l_s[...] = alpha * l_s[...] + jnp.sum(pr, axis=1, keepdims=True)
        pbd = jnp.concatenate([jnp.where(grp == g, pr, 0.0) for g in range(KV_A)], axis=1).astype(BF16)
        o_s[...] = alpha * o_s[...] + _dot(pbd, vstack.astype(BF16))
        m_s[...] = m_new

    thr = thr_ref[:, 0:1]
    kflat = jnp.concatenate([jnp.concatenate([k[:, g, :] for g in range(KV_A)], axis=1) for k in k_refs], axis=0)
    vstack = jnp.concatenate([v[:, g, :] for g in range(KV_A) for v in v_refs], axis=0)
    update(_sortable(i_ref[...]) >= thr, kflat, vstack)

    @pl.when(p == n_groups - 1)
    def _():
        t = lax.broadcasted_iota(I32, (n_new, LANES), 0)
        s = lax.broadcasted_iota(I32, (n_new, LANES), 1)
        sel_new = (_sortable(inew_ref[:, 0:LANES]) >= thr) & (s <= t) & (s < n_new)
        pad = jnp.zeros((LANES - n_new, KV_A * HEAD_DIM), F32)
        kn = jnp.concatenate([knew_ref[...], pad], axis=0)
        vn = jnp.concatenate([vnew_ref[...], pad], axis=0)
        vns = jnp.concatenate([vn[:, g * HEAD_DIM:(g + 1) * HEAD_DIM] for g in range(KV_A)], axis=0)
        update(sel_new, kn, vns)
        o = o_s[...] / l_s[...]
        for h in range(H_A):
            o_ref[:, h * HEAD_DIM:(h + 1) * HEAD_DIM] = o[h * n_new:(h + 1) * n_new, :]


def _dsa_sample_attend(hp, scores, thr, cache_k_a, cache_v_a, pt_flat, off, layer, mp, dec_batch, n_new, n_pages):
    kvw = KV_A * HEAD_DIM
    rb = mp // n_new
    n_rows = H_A * n_new
    n_groups = n_pages // PAGE_GROUP
    gw = PAGE_GROUP * LANES
    page_block = (None, None, LANES, KV_A, HEAD_DIM)
    grid_spec = pltpu.PrefetchScalarGridSpec(
        num_scalar_prefetch=1,
        grid=(dec_batch, n_groups),
        in_specs=[pl.BlockSpec((n_new, A_W), lambda b, p, pt: (rb + b, off["q_a"] // A_W)),
                  pl.BlockSpec((None, None, n_new, gw), lambda b, p, pt: (b, p, 0, 0)),
                  pl.BlockSpec((None, None, n_new, gw), lambda b, p, pt: (b, n_groups, 0, 0)),
                  pl.BlockSpec((n_new, LANES), lambda b, p, pt: (b, 0)),
                  pl.BlockSpec((n_new, kvw), lambda b, p, pt: (rb + b, off["k_a"] // kvw)),
                  pl.BlockSpec((n_new, kvw), lambda b, p, pt: (rb + b, off["v_a"] // kvw))]
        + _page_specs(page_block, n_pages, layer) + _page_specs(page_block, n_pages, layer),
        out_specs=pl.BlockSpec((n_new, A_W), lambda b, p, pt: (b, 0)),
        scratch_shapes=[pltpu.VMEM((n_rows, KV_A * HEAD_DIM), BF16),
                        pltpu.VMEM((n_rows, HEAD_DIM), F32),
                        pltpu.VMEM((n_rows, 1), F32),
                        pltpu.VMEM((n_rows, 1), F32)])
    return pl.pallas_call(
        functools.partial(_dsa_sample_attend_kernel, n_groups=n_groups, n_new=n_new),
        grid_spec=grid_spec,
        out_shape=jax.ShapeDtypeStruct((dec_batch * n_new, A_W), F32),
        compiler_params=_cp(("arbitrary", "arbitrary")),
        name="dsa_sample_attend",
    )(pt_flat, hp, scores, scores, thr, hp, hp, *([cache_k_a] * PAGE_GROUP), *([cache_v_a] * PAGE_GROUP))


def _stick_sample_kernel(pt_ref, q_ref, knew_ref, vnew_ref, *rest, n_groups, n_new):
    del pt_ref
    k_refs, v_refs = rest[:PAGE_GROUP], rest[PAGE_GROUP:2 * PAGE_GROUP]
    o_ref, qbd_s, acc_s, tot_s, done_s = rest[2 * PAGE_GROUP:]
    p = pl.program_id(1)
    n_rows = H_C * n_new
    head = lax.broadcasted_iota(I32, (n_rows, LANES), 0) // n_new
    u2 = _suffix_matrix()
    scale = HEAD_DIM ** -0.5

    def sweep(mask, kflat, vstack):
        z = _dot_nt(qbd_s[...], kflat.astype(BF16)) * scale
        lk, ls = _log_sig_pair(z)
        if mask is not None:
            lk = jnp.where(mask, lk, 0.0)
        within, total = _suffix_sums(lk, u2)
        a = jnp.exp(ls + within + tot_s[...])
        if mask is not None:
            a = jnp.where(mask, a, 0.0)
        abd = jnp.concatenate([jnp.where(head == h, a, 0.0) for h in range(H_C)], axis=1).astype(BF16)
        acc_s[...] = acc_s[...] + _dot(abd, vstack.astype(BF16))
        tot_s[...] = tot_s[...] + total

    @pl.when(p == 0)
    def _():
        qs = jnp.concatenate([q_ref[:, h * HEAD_DIM:(h + 1) * HEAD_DIM] for h in range(H_C)], axis=0)
        qbd_s[...] = jnp.concatenate([jnp.where(head == h, qs, 0.0) for h in range(H_C)], axis=1).astype(BF16)
        acc_s[...] = jnp.zeros(acc_s.shape, F32)
        tot_s[...] = jnp.zeros(tot_s.shape, F32)
        done_s[0] = 0
        t = lax.broadcasted_iota(I32, (n_rows, LANES), 0) % n_new
        s = lax.broadcasted_iota(I32, (n_rows, LANES), 1)
        pad = jnp.zeros((LANES - n_new, C_W), F32)
        kn = jnp.concatenate([knew_ref[...], pad], axis=0)
        vn = jnp.concatenate([vnew_ref[...], pad], axis=0)
        vns = jnp.concatenate([vn[:, h * HEAD_DIM:(h + 1) * HEAD_DIM] for h in range(H_C)], axis=0)
        sweep(s < t, kn, vns)

    @pl.when(done_s[0] == 0)
    def _():
        for k_ref, v_ref in zip(k_refs, v_refs):
            kflat = jnp.concatenate([k_ref[h] for h in range(H_C)], axis=1)
            vstack = jnp.concatenate([v_ref[h] for h in range(H_C)], axis=0)
            sweep(None, kflat, vstack)
        done_s[0] = (jnp.max(tot_s[...]) < STICK_EXIT).astype(I32)

    @pl.when(p == n_groups - 1)
    def _():
        acc = acc_s[...]
        for h in range(H_C):
            o_ref[:, h * HEAD_DIM:(h + 1) * HEAD_DIM] = acc[h * n_new:(h + 1) * n_new, :]


def _stick_sample(hp, cache_k_c_t, cache_v_c_t, pt_flat, off, layer, mp, dec_batch, n_new, n_pages):
    rb = mp // n_new
    n_rows = H_C * n_new
    n_groups = n_pages // PAGE_GROUP
    page_block = (None, None, H_C, LANES, HEAD_DIM)
    grid_spec = pltpu.PrefetchScalarGridSpec(
        num_scalar_prefetch=1,
        grid=(dec_batch, n_groups),
        in_specs=[pl.BlockSpec((n_new, C_W), lambda b, p, pt: (rb + b, off["q_c"] // C_W)),
                  pl.BlockSpec((n_new, C_W), lambda b, p, pt: (rb + b, off["k_c"] // C_W)),
                  pl.BlockSpec((n_new, C_W), lambda b, p, pt: (rb + b, off["v_c"] // C_W))]
        + _page_specs(page_block, n_pages, layer, reverse=True)
        + _page_specs(page_block, n_pages, layer, reverse=True),
        out_specs=pl.BlockSpec((n_new, C_W), lambda b, p, pt: (b, 0)),
        scratch_shapes=[pltpu.VMEM((n_rows, C_W), BF16),
                        pltpu.VMEM((n_rows, HEAD_DIM), F32),
                        pltpu.VMEM((n_rows, LANES), F32),
                        pltpu.SMEM((1,), I32)])
    return pl.pallas_call(
        functools.partial(_stick_sample_kernel, n_groups=n_groups, n_new=n_new),
        grid_spec=grid_spec,
        out_shape=jax.ShapeDtypeStruct((dec_batch * n_new, C_W), F32),
        compiler_params=_cp(("arbitrary", "arbitrary")),
        name="stick_sample",
    )(pt_flat, hp, hp, hp, *([cache_k_c_t] * PAGE_GROUP), *([cache_v_c_t] * PAGE_GROUP))


def _merge_kernel(oa_ref, ob_ref, oc_ref, ga_ref, gb_ref, gc_ref, wpa_ref, wpb_ref, wpc_ref, m_ref):
    m = (jax.nn.sigmoid(ga_ref[...]) * _dot(oa_ref[...], wpa_ref[...])
         + jax.nn.sigmoid(gb_ref[...]) * _dot(ob_ref[...], wpb_ref[...])
         + jax.nn.sigmoid(gc_ref[...]) * _dot(oc_ref[...], wpc_ref[...]))
    m_ref[...] = m.astype(m_ref.dtype)


def _merge(o_a, o_b, o_c, hp, off, w_pa, w_pb, w_pc, layer):
    mt = hp.shape[0]
    d = w_pa.shape[2]
    gate = lambda name: pl.BlockSpec((TM, d), lambda i: (i, off[name] // d))
    wspec = lambda k: pl.BlockSpec((None, k, d), lambda i: (layer, 0, 0))
    return pl.pallas_call(
        _merge_kernel,
        grid=(mt // TM,),
        in_specs=[pl.BlockSpec((TM, A_W), lambda i: (i, 0)),
                  pl.BlockSpec((TM, W_B), lambda i: (i, 0)),
                  pl.BlockSpec((TM, C_W), lambda i: (i, 0)),
                  gate("g_a"), gate("g_b"), gate("g_c"),
                  wspec(A_W), wspec(W_B), wspec(C_W)],
        out_specs=pl.BlockSpec((TM, d), lambda i: (i, 0)),
        out_shape=jax.ShapeDtypeStruct((mt, d), BF16),
        compiler_params=_cp(("arbitrary",)),
        name="merge",
    )(o_a, o_b, o_c, hp, hp, hp, w_pa, w_pb, w_pc)


def _post_mix_kernel(m_ref, x_ref, wout_ref, g_ref, b_ref, wr_ref, br_ref,
                     x1_ref, x1b_ref, idx_ref, wsel_ref, *, alpha, n_valid, n_experts):
    x1 = _layer_norm(alpha * x_ref[...] + _dot(m_ref[...], wout_ref[...]), g_ref[...], b_ref[...])
    x1_ref[...] = x1
    x1b = x1.astype(BF16)
    x1b_ref[...] = x1b
    scores = jax.nn.sigmoid(_dot(x1b, wr_ref[...]))
    lane = lax.broadcasted_iota(I32, scores.shape, 1)
    lane_f = lane.astype(F32)
    sel = jnp.where(lane < n_experts, scores + br_ref[...], -jnp.inf)
    idx = jnp.full(scores.shape, -1.0, F32)
    wts = jnp.zeros(scores.shape, F32)
    for k in range(TOP_K):
        mx = jnp.max(sel, axis=1, keepdims=True)
        ik = jnp.min(jnp.where(sel == mx, lane_f, float(LANES)), axis=1, keepdims=True)
        hit = lane_f == ik
        wk = jnp.sum(jnp.where(hit, scores, 0.0), axis=1, keepdims=True)
        sel = jnp.where(hit, -jnp.inf, sel)
        idx = jnp.where(lane == k, ik, idx)
        wts = jnp.where(lane == k, wk, wts)
    wts = wts / jnp.sum(wts, axis=1, keepdims=True) * ROUTED_SCALE
    rowid = pl.program_id(0) * TM + lax.broadcasted_iota(I32, scores.shape, 0)
    valid = rowid < n_valid
    idx_ref[...] = jnp.where(valid, idx, -1.0).astype(I32)
    wsel_ref[...] = jnp.where(valid, wts, 0.0)


def _post_mix(m, xf, w_out, g, b, w_r, b_r, layer, alpha, n_valid, n_experts):
    mt, d = xf.shape
    row = lambda i: (i, 0)
    vec = lambda n: pl.BlockSpec((None, 1, n), lambda i: (layer, 0, 0))
    return pl.pallas_call(
        functools.partial(_post_mix_kernel, alpha=alpha, n_valid=n_valid, n_experts=n_experts),
        grid=(mt // TM,),
        in_specs=[pl.BlockSpec((TM, d), row), pl.BlockSpec((TM, d), row),
                  pl.BlockSpec((None, d, d), lambda i: (layer, 0, 0)),
                  vec(d), vec(d),
                  pl.BlockSpec((None, d, LANES), lambda i: (layer, 0, 0)),
                  vec(LANES)],
        out_specs=[pl.BlockSpec((TM, d), row), pl.BlockSpec((TM, d), row),
                   pl.BlockSpec((TM, LANES), row), pl.BlockSpec((TM, LANES), row)],
        out_shape=[jax.ShapeDtypeStruct((mt, d), F32), jax.ShapeDtypeStruct((mt, d), BF16),
                   jax.ShapeDtypeStruct((mt, LANES), I32), jax.ShapeDtypeStruct((mt, LANES), F32)],
        compiler_params=_cp(("arbitrary",)),
        name="post_mix",
    )(m, xf, w_out, g, b, w_r, b_r)


def _route_kernel(idx_ref, pos_ref, be_ref, nu_ref, *, mt, n_experts, nbp):
    n_tiles = mt // ROUTE_TR
    lane = lax.broadcasted_iota(I32, (ROUTE_TR, LANES), 1)

    def one_hot(t):
        idx = idx_ref[pl.ds(pl.multiple_of(t * ROUTE_TR, ROUTE_TR), ROUTE_TR), :]
        oh = jnp.zeros((ROUTE_TR, LANES), F32)
        for k in range(TOP_K):
            oh = oh + jnp.where(idx[:, k:k + 1] == lane, 1.0, 0.0)
        return idx, oh

    def count_body(t, cnt):
        return cnt + jnp.sum(one_hot(t)[1], axis=0, keepdims=True)

    cnt = lax.fori_loop(0, n_tiles, count_body, jnp.zeros((1, LANES), F32))
    nb = jnp.floor((cnt + (EXPERT_BLK - 1)) * (1.0 / EXPERT_BLK))
    r = lax.broadcasted_iota(I32, (LANES, LANES), 0)
    c = lax.broadcasted_iota(I32, (LANES, LANES), 1)
    nb8 = jnp.broadcast_to(nb, (8, LANES))
    start = _dot(nb8.astype(BF16), jnp.where(r < c, 1.0, 0.0).astype(BF16))
    pend = start + nb8
    start_row = start[0:1, :] * float(EXPERT_BLK)

    rr = lax.broadcasted_iota(I32, (ROUTE_TR, ROUTE_TR), 0)
    cc = lax.broadcasted_iota(I32, (ROUTE_TR, ROUTE_TR), 1)
    lower = jnp.where(cc < rr, 1.0, 0.0).astype(BF16)

    def pos_body(t, seen):
        idx, oh = one_hot(t)
        pe = start_row + seen + _dot(lower, oh.astype(BF16))
        pos = jnp.zeros((ROUTE_TR, LANES), F32)
        for k in range(TOP_K):
            pk = jnp.sum(jnp.where(idx[:, k:k + 1] == lane, pe, 0.0), axis=1, keepdims=True)
            pos = jnp.where(lane == k, pk, pos)
        pos_ref[pl.ds(pl.multiple_of(t * ROUTE_TR, ROUTE_TR), ROUTE_TR), :] = pos.astype(I32)
        return seen + jnp.sum(oh, axis=0, keepdims=True)

    lax.fori_loop(0, n_tiles, pos_body, jnp.zeros((1, LANES), F32))

    pend_t = jnp.broadcast_to(pend[0:1, :], (LANES, LANES)).T
    for ch in range(nbp // LANES):
        blk = (ch * LANES + c).astype(F32)
        be = jnp.sum(jnp.where((pend_t <= blk) & (r < n_experts), 1.0, 0.0), axis=0, keepdims=True)
        be = jnp.minimum(be, float(n_experts - 1))
        be_ref[:, ch * LANES:(ch + 1) * LANES] = jnp.broadcast_to(be, (8, LANES)).astype(I32)
    nu_ref[...] = jnp.broadcast_to(jnp.max(pend, axis=1, keepdims=True), (8, LANES)).astype(I32)


def _route(idx, n_experts, n_blocks):
    mt = idx.shape[0]
    nbp = -(-n_blocks // LANES) * LANES
    return pl.pallas_call(
        functools.partial(_route_kernel, mt=mt, n_experts=n_experts, nbp=nbp),
        out_shape=[jax.ShapeDtypeStruct((mt, LANES), I32),
                   jax.ShapeDtypeStruct((8, nbp), I32),
                   jax.ShapeDtypeStruct((8, LANES), I32)],
        compiler_params=pltpu.CompilerParams(vmem_limit_bytes=VMEM_LIMIT),
        name="route",
    )(idx)


def _dispatch_kernel(pos_ref, x_ref, xs_in_ref, xs_ref, sem, *, n_valid):
    del xs_in_ref
    t0 = pl.program_id(0) * DISPATCH_TD
    n = jnp.clip(n_valid - t0, 0, DISPATCH_TD)

    def copy(r, k):
        dst = pos_ref[(t0 + r) * TOP_K + k]
        return pltpu.make_async_copy(x_ref.at[pl.ds(r, 1), :], xs_ref.at[pl.ds(dst, 1), :], sem)

    def start_body(r, c):
        for k in range(TOP_K):
            copy(r, k).start()
        return c

    def wait_body(r, c):
        for k in range(TOP_K):
            copy(r, k).wait()
        return c

    lax.fori_loop(0, n, start_body, 0)

    @pl.when(n == DISPATCH_TD)
    def _():
        for _ in range(TOP_K):
            pltpu.make_async_copy(x_ref, xs_ref.at[pl.ds(0, DISPATCH_TD), :], sem).wait()

    @pl.when(n != DISPATCH_TD)
    def _():
        lax.fori_loop(0, n, wait_body, 0)


def _dispatch(pos_flat, x1, n_rows, n_valid):
    mt, d = x1.shape
    grid_spec = pltpu.PrefetchScalarGridSpec(
        num_scalar_prefetch=1,
        grid=(mt // DISPATCH_TD,),
        in_specs=[pl.BlockSpec((DISPATCH_TD, d), lambda i, pos: (i, 0)),
                  pl.BlockSpec(memory_space=pl.ANY)],
        out_specs=pl.BlockSpec(memory_space=pl.ANY),
        scratch_shapes=[pltpu.SemaphoreType.DMA(())])
    return pl.pallas_call(
        functools.partial(_dispatch_kernel, n_valid=n_valid),
        grid_spec=grid_spec,
        out_shape=jax.ShapeDtypeStruct((n_rows, d), F32),
        input_output_aliases={2: 0},
        compiler_params=_cp(("arbitrary",)),
        name="dispatch",
    )(pos_flat, x1, jnp.zeros((n_rows, d), F32))


def _expert_kernel(be_ref, nu_ref, x_ref, wg_ref, wu_ref, wd_ref, y_ref, wg_s, wu_s, wd_s):
    i = pl.program_id(0)
    used = i < nu_ref[0]
    new_expert = (i == 0) | (be_ref[i] != be_ref[jnp.maximum(i - 1, 0)])

    @pl.when(used & new_expert)
    def _():
        wg_s[...] = wg_ref[...].astype(BF16)
        wu_s[...] = wu_ref[...].astype(BF16)
        wd_s[...] = wd_ref[...].astype(BF16)

    @pl.when(used)
    def _():
        x = x_ref[...].astype(BF16)
        a = jax.nn.silu(_dot(x, wg_s[...])) * _dot(x, wu_s[...])
        y_ref[...] = _dot(a.astype(BF16), wd_s[...])

    @pl.when(jnp.logical_not(used))
    def _():
        y_ref[...] = jnp.zeros(y_ref.shape, F32)


def _experts(xs, be, nu, w_gate, w_up, w_down, layer, n_blocks):
    n_rows, d = xs.shape
    f = w_gate.shape[3]
    blk = lambda i, be, nu: (jnp.minimum(i, nu[0] - 1), 0)
    grid_spec = pltpu.PrefetchScalarGridSpec(
        num_scalar_prefetch=2,
        grid=(n_blocks,),
        in_specs=[pl.BlockSpec((EXPERT_BLK, d), blk),
                  pl.BlockSpec((None, None, d, f), lambda i, be, nu: (layer, be[i], 0, 0)),
                  pl.BlockSpec((None, None, d, f), lambda i, be, nu: (layer, be[i], 0, 0)),
                  pl.BlockSpec((None, None, f, d), lambda i, be, nu: (layer, be[i], 0, 0))],
        out_specs=pl.BlockSpec((EXPERT_BLK, d), lambda i, be, nu: (i, 0)),
        scratch_shapes=[pltpu.VMEM((d, f), BF16), pltpu.VMEM((d, f), BF16), pltpu.VMEM((f, d), BF16)])
    return pl.pallas_call(
        _expert_kernel,
        grid_spec=grid_spec,
        out_shape=jax.ShapeDtypeStruct((n_rows, d), F32),
        compiler_params=_cp(("arbitrary",)),
        name="experts",
    )(be, nu, xs, w_gate, w_up, w_down)


def _combine_kernel(pos_ref, ys_ref, wsel_ref, x1_ref, x1b_ref, wsg_ref, wsu_ref, wsd_ref, g_ref, b_ref,
                    x2_ref, x2b_ref, buf, sem, *, alpha):
    t0 = pl.program_id(0) * COMBINE_TC

    def copy(r, k):
        src = pos_ref[(t0 + r) * TOP_K + k]
        return pltpu.make_async_copy(ys_ref.at[pl.ds(src, 1), :], buf.at[pl.ds(k * COMBINE_TC + r, 1), :], sem)

    def start_body(r, c):
        for k in range(TOP_K):
            copy(r, k).start()
        return c

    lax.fori_loop(0, COMBINE_TC, start_body, 0)
    x1b = x1b_ref[...]
    shared = _dot((jax.nn.silu(_dot(x1b, wsg_ref[...])) * _dot(x1b, wsu_ref[...])).astype(BF16), wsd_ref[...])
    pltpu.make_async_copy(ys_ref.at[pl.ds(0, TOP_K * COMBINE_TC), :], buf, sem).wait()
    wsel = wsel_ref[...]
    routed = wsel[:, 0:1] * buf[0:COMBINE_TC, :]
    for k in range(1, TOP_K):
        routed = routed + wsel[:, k:k + 1] * buf[k * COMBINE_TC:(k + 1) * COMBINE_TC, :]
    x2 = _layer_norm(alpha * x1_ref[...] + (routed + shared), g_ref[...], b_ref[...])
    x2_ref[...] = x2
    x2b_ref[...] = x2.astype(BF16)


def _combine(pos_flat, ys, wsel, x1, x1b, ws_gate, ws_up, ws_down, g, b, layer, alpha):
    mt, d = x1.shape
    f = ws_gate.shape[2]
    row = lambda i, pos: (i, 0)
    vec = pl.BlockSpec((None, 1, d), lambda i, pos: (layer, 0, 0))
    grid_spec = pltpu.PrefetchScalarGridSpec(
        num_scalar_prefetch=1,
        grid=(mt // COMBINE_TC,),
        in_specs=[pl.BlockSpec(memory_space=pl.ANY),
                  pl.BlockSpec((COMBINE_TC, LANES), row),
                  pl.BlockSpec((COMBINE_TC, d), row),
                  pl.BlockSpec((COMBINE_TC, d), row),
                  pl.BlockSpec((None, d, f), lambda i, pos: (layer, 0, 0)),
                  pl.BlockSpec((None, d, f), lambda i, pos: (layer, 0, 0)),
                  pl.BlockSpec((None, f, d), lambda i, pos: (layer, 0, 0)),
                  vec, vec],
        out_specs=[pl.BlockSpec((COMBINE_TC, d), row), pl.BlockSpec((COMBINE_TC, d), row)],
        scratch_shapes=[pltpu.VMEM((TOP_K * COMBINE_TC, d), F32), pltpu.SemaphoreType.DMA(())])
    return pl.pallas_call(
        functools.partial(_combine_kernel, alpha=alpha),
        grid_spec=grid_spec,
        out_shape=[jax.ShapeDtypeStruct((mt, d), F32), jax.ShapeDtypeStruct((mt, d), BF16)],
        compiler_params=_cp(("arbitrary",)),
        name="combine",
    )(pos_flat, ys, wsel, x1, x1b, ws_gate, ws_up, ws_down, g, b)


def kernel(x_prompt, x_sample, cache_k_a, cache_v_a, cache_idx_k, cache_k_c, cache_v_c, page_table, w_in, w_s, b_s, ln_v_g, ln_v_b, w_pa, w_pb, w_pc, w_out, ln1_g, ln1_b, w_router, b_router, w_gate, w_up, w_down, ws_gate, ws_up, ws_down, ln2_g, ln2_b):
    batch, seq, d = x_prompt.shape
    dec_batch, n_new, _ = x_sample.shape
    depth = w_in.shape[0]
    n_pages = page_table.shape[1]
    n_experts = w_router.shape[2]
    alpha = float((2 * depth) ** 0.25)
    mp, ms = batch * seq, dec_batch * n_new
    n_valid = mp + ms
    mt = -(-n_valid // TM) * TM
    assert seq % QB == 0 and n_new == 8 and n_experts <= LANES and mp % TM == 0
    n_blocks = -(-(n_valid * TOP_K + n_experts * (EXPERT_BLK - 1)) // EXPERT_BLK)
    n_rows = n_blocks * EXPERT_BLK

    pieces, src, off, n_cols = _layout(d)
    cols = []
    for name, w in pieces:
        s0, sw = src[name]
        piece = w_in[:, :, s0:s0 + sw]
        if sw < w:
            piece = jnp.pad(piece, ((0, 0), (0, 0), (0, w - sw)))
        cols.append(piece)
    w_in_r = jnp.concatenate(cols, axis=2).astype(BF16)

    bf = lambda a: a.astype(BF16)
    w_pa_b, w_pb_b, w_pc_b, w_out_b = bf(w_pa), bf(w_pb), bf(w_pc), bf(w_out)
    ws_gate_b, ws_up_b, ws_down_b = bf(ws_gate), bf(ws_up), bf(ws_down)
    w_r_b = bf(jnp.pad(w_router, ((0, 0), (0, 0), (0, LANES - n_experts))))
    b_r = jnp.pad(b_router, ((0, 0), (0, LANES - n_experts)))[:, None, :]
    b_s_t = jnp.swapaxes(b_s, 1, 2)
    vec3 = lambda a: a[:, None, :]
    ln1_g3, ln1_b3, ln2_g3, ln2_b3 = vec3(ln1_g), vec3(ln1_b), vec3(ln2_g), vec3(ln2_b)
    pt_flat = page_table.reshape(-1).astype(I32)
    assert n_pages % PAGE_GROUP == 0
    cache_k_c_t = jnp.swapaxes(cache_k_c, 2, 3)
    cache_v_c_t = jnp.swapaxes(cache_v_c, 2, 3)

    xf = jnp.concatenate([x_prompt.reshape(mp, d), x_sample.reshape(ms, d),
                          jnp.zeros((mt - n_valid, d), F32)], axis=0)
    xb = xf.astype(BF16)

    st = [[] for _ in range(11)]
    for l in range(depth):
        hp = _in_proj(xb, w_in_r, l)

        def cut(name, lo, hi, width):
            return hp[lo:hi, off[name]:off[name] + width]

        o_a_p = _dsa_prompt(hp, off, batch, seq)
        o_c_p = _stick_prompt(hp, off, batch, seq)
        o_b_p, _ = _gmlp(hp, hp, off["u_b"] // W_B, off["v_b"] // W_B, mp // CHUNK,
                         w_s[l], b_s_t[l], ln_v_g[l][None], ln_v_b[l][None])

        iw_s = cut("iw", mp, mp + ms, IDX_H).reshape(dec_batch, n_new, IDX_H)
        wrow = jnp.swapaxes(iw_s, 1, 2).reshape(dec_batch, 1, IDX_H * n_new)
        wrow = jnp.broadcast_to(wrow, (dec_batch, n_new, IDX_H * n_new)).reshape(ms, IDX_H * n_new)
        scores, thr = _dsa_sample_scores(hp, wrow, cache_idx_k, pt_flat, off, l, mp, dec_batch, n_new, n_pages)
        o_a_s = _dsa_sample_attend(hp, scores, thr, cache_k_a, cache_v_a, pt_flat, off, l, mp,
                                   dec_batch, n_new, n_pages)
        o_c_s = _stick_sample(hp, cache_k_c_t, cache_v_c_t, pt_flat, off, l, mp, dec_batch, n_new, n_pages)
        pad_chunk = lambda a: jnp.pad(a.reshape(dec_batch, n_new, W_B),
                                      ((0, 0), (0, CHUNK - n_new), (0, 0))).reshape(dec_batch * CHUNK, W_B)
        o_b_s, vn_s = _gmlp(pad_chunk(cut("u_b", mp, mp + ms, W_B)), pad_chunk(cut("v_b", mp, mp + ms, W_B)),
                            0, 0, dec_batch, w_s[l], b_s_t[l], ln_v_g[l][None], ln_v_b[l][None])
        unpad = lambda a: a.reshape(dec_batch, CHUNK, W_B)[:, :n_new].reshape(ms, W_B)

        tail = lambda w, dt: jnp.zeros((mt - n_valid, w), dt)
        o_a = jnp.concatenate([o_a_p, o_a_s.astype(BF16), tail(A_W, BF16)], axis=0)
        o_b = jnp.concatenate([o_b_p, unpad(o_b_s), tail(W_B, BF16)], axis=0)
        o_c = jnp.concatenate([o_c_p, o_c_s.astype(BF16), tail(C_W, BF16)], axis=0)

        m = _merge(o_a, o_b, o_c, hp, off, w_pa_b, w_pb_b, w_pc_b, l)
        x1, x1b, idx, wsel = _post_mix(m, xf, w_out_b, ln1_g3, ln1_b3, w_r_b, b_r, l, alpha, n_valid, n_experts)

        pos, be, nu = _route(idx, n_experts, n_blocks)
        pos_flat = pos[:, :TOP_K].reshape(-1)
        xs = _dispatch(pos_flat, x1, n_rows, n_valid)
        ys = _experts(xs, be[0, :n_blocks], nu[0, :1], w_gate, w_up, w_down, l, n_blocks)
        xf, xb = _combine(pos_flat, ys, wsel, x1, x1b, ws_gate_b, ws_up_b, ws_down_b, ln2_g3, ln2_b3, l, alpha)

        kvw = KV_A * HEAD_DIM
        for lst, (name, w, shp) in zip(st[:5], (("k_a", kvw, (KV_A, HEAD_DIM)), ("v_a", kvw, (KV_A, HEAD_DIM)),
                                                ("ik", IDX_DIM, (IDX_DIM,)), ("k_c", C_W, (H_C, HEAD_DIM)),
                                                ("v_c", C_W, (H_C, HEAD_DIM)))):
            lst.append(cut(name, 0, mp, w).reshape(batch, seq, *shp))
        for lst, (name, w, shp) in zip(st[5:10], (("k_a", kvw, (KV_A, HEAD_DIM)), ("v_a", kvw, (KV_A, HEAD_DIM)),
                                                  ("ik", IDX_DIM, (IDX_DIM,)), ("k_c", C_W, (H_C, HEAD_DIM)),
                                                  ("v_c", C_W, (H_C, HEAD_DIM)))):
            lst.append(cut(name, mp, mp + ms, w).reshape(dec_batch, n_new, *shp))
        st[10].append(unpad(vn_s).reshape(dec_batch, n_new, W_B))

    y_prompt = xf[:mp].reshape(batch, seq, d)
    y_sample = xf[mp:mp + ms].reshape(dec_batch, n_new, d)
    return (y_prompt, y_sample) + tuple(jnp.stack(s) for s in st)
```

```python
import functools

import jax
import jax.numpy as jnp
from jax import lax
from jax.experimental import pallas as pl
from jax.experimental.pallas import tpu as pltpu

F32 = jnp.float32
BF16 = jnp.bfloat16
I32 = jnp.int32

HEAD_DIM = 128
H_A = 6
KV_A = 2
IDX_H = 16
IDX_DIM = 128
TOPK_MAX = 256
CHUNK = 128
G_B = 4
W_B = 512
H_C = 6
QB = 128
TOP_K = 6
ROUTED_SCALE = 2.5
LN_EPS = 1e-5
A_W = H_A * HEAD_DIM
C_W = H_C * HEAD_DIM
LANES = 128
TM = 512
EXPERT_BLK = 256
ROUTE_TR = 256
DISPATCH_TD = 256
COMBINE_TC = 128
DSA_KT = 256
STICK_HG = 3
STICK_EXIT = -110.0
PAGE_GROUP = 8
NEG = -1e30
INT_MIN = -2147483648
VMEM_LIMIT = 56 * 1024 * 1024


def _cp(sem, vmem=VMEM_LIMIT):
    return pltpu.CompilerParams(dimension_semantics=sem, vmem_limit_bytes=vmem)


def _layout(d_model):
    pieces = (("iq", IDX_H * IDX_DIM), ("g_a", d_model), ("g_b", d_model), ("g_c", d_model),
              ("k_a", KV_A * HEAD_DIM), ("q_a", A_W), ("q_c", C_W), ("k_c", C_W), ("v_c", C_W),
              ("v_a", KV_A * HEAD_DIM), ("u_b", W_B), ("v_b", W_B), ("ik", IDX_DIM), ("iw", LANES))
    src_order = (("q_a", A_W), ("k_a", KV_A * HEAD_DIM), ("v_a", KV_A * HEAD_DIM), ("iq", IDX_H * IDX_DIM),
                 ("ik", IDX_DIM), ("iw", IDX_H), ("u_b", W_B), ("v_b", W_B), ("q_c", C_W), ("k_c", C_W),
                 ("v_c", C_W), ("g_a", d_model), ("g_b", d_model), ("g_c", d_model))
    src, acc = {}, 0
    for name, w in src_order:
        src[name] = (acc, w)
        acc += w
    off, acc = {}, 0
    for name, w in pieces:
        assert acc % w == 0, (name, acc, w)
        off[name] = acc
        acc += w
    return pieces, src, off, acc


def _sortable(x):
    bits = lax.bitcast_convert_type(x, I32)
    return jnp.where(bits < 0, bits ^ 0x7FFFFFFF, bits)


def _gelu(x):
    return 0.5 * x * (1.0 + lax.erf(x * (2.0 ** -0.5)))


def _layer_norm(y, g, b):
    mu = jnp.mean(y, axis=-1, keepdims=True)
    yc = y - mu
    var = jnp.mean(yc * yc, axis=-1, keepdims=True)
    return yc * lax.rsqrt(var + LN_EPS) * g + b


def _dot(a, b):
    return jnp.dot(a, b, preferred_element_type=F32)


def _dot_nt(a, b):
    return lax.dot_general(a, b, (((1,), (1,)), ((), ())), preferred_element_type=F32)


def _pack_halves(xb):
    half = xb.shape[1] // 2
    bits = lax.bitcast_convert_type(xb.astype(F32), I32)
    return lax.shift_right_logical(bits[:, :half], 16) | (bits[:, half:] & jnp.int32(-65536))


def _unpack_halves(u):
    lo = lax.bitcast_convert_type(lax.shift_left(u, 16), F32).astype(BF16)
    hi = lax.bitcast_convert_type(u & jnp.int32(-65536), F32).astype(BF16)
    return lo, hi


def _log_sig_pair(z):
    sp = jnp.log1p(jnp.exp(-jnp.abs(z)))
    return -(jnp.maximum(z, 0.0) + sp), -(jnp.maximum(-z, 0.0) + sp)


def _suffix_matrix():
    r = lax.broadcasted_iota(I32, (LANES, 2 * LANES), 0)
    c = lax.broadcasted_iota(I32, (LANES, 2 * LANES), 1)
    return jnp.where((c >= LANES) | (r > c), 1.0, 0.0).astype(BF16)


def _suffix_sums(lk, u2):
    hi = lk.astype(BF16)
    lo = (lk - hi.astype(F32)).astype(BF16)
    cs = _dot(hi, u2) + _dot(lo, u2)
    return cs[:, :LANES], cs[:, LANES:]


def _matmul_kernel(x_ref, w_ref, o_ref):
    o_ref[...] = _dot(x_ref[...], w_ref[...])


def _in_proj(xb, w_r, layer):
    mt, d = xb.shape
    n = w_r.shape[2]
    tn = max(t for t in range(LANES, 1025, LANES) if n % t == 0)
    return pl.pallas_call(
        _matmul_kernel,
        grid=(n // tn, mt // TM),
        in_specs=[pl.BlockSpec((TM, d), lambda j, i: (i, 0)),
                  pl.BlockSpec((None, d, tn), lambda j, i: (layer, 0, j))],
        out_specs=pl.BlockSpec((TM, tn), lambda j, i: (i, j)),
        out_shape=jax.ShapeDtypeStruct((mt, n), F32),
        compiler_params=_cp(("arbitrary", "arbitrary")),
        name="in_proj",
    )(xb, w_r)


def _dsa_prompt_kernel(q_ref, iq_ref, iw_ref, k_ref, v_ref, ik_ref, o_ref,
                       iqt_s, key_s, vt_s, qb_s, ot_s, *, seq, topk):
    i = pl.program_id(1)
    n_t = (i * QB + QB + DSA_KT - 1) // DSA_KT

    def key_rows(j):
        return pl.ds(pl.multiple_of(j * DSA_KT, DSA_KT), DSA_KT)

    @pl.when(i == 0)
    def _():
        def body(j, c):
            for g in range(KV_A):
                vt_s[g, j] = v_ref[key_rows(j), g * HEAD_DIM:(g + 1) * HEAD_DIM].T.astype(BF16)
            return c
        lax.fori_loop(0, seq // DSA_KT, body, 0)

    for h in range(IDX_H):
        iqt_s[:, h * QB:(h + 1) * QB] = iq_ref[:, h * IDX_DIM:(h + 1) * IDX_DIM].T.astype(BF16)
    qb_s[...] = q_ref[...].astype(BF16)
    iwt = iw_ref[...].T * ((IDX_DIM ** -0.5) * (IDX_H ** -0.5))

    row = lax.broadcasted_iota(I32, (DSA_KT, QB), 0)
    col = lax.broadcasted_iota(I32, (DSA_KT, QB), 1)

    def causal(j):
        return (j * DSA_KT + row) <= (i * QB + col)

    def score_body(j, c):
        ikt = ik_ref[key_rows(j), :].astype(BF16)
        s_all = _dot(ikt, iqt_s[...])
        acc = jnp.zeros((DSA_KT, QB), F32)
        for h in range(IDX_H):
            acc = acc + jnp.maximum(s_all[:, h * QB:(h + 1) * QB], 0.0) * iwt[h:h + 1, :]
        key_s[j] = _sortable(jnp.where(causal(j), acc, -jnp.inf))
        return c

    lax.fori_loop(0, n_t, score_body, 0)

    def bit_body(bi, tu):
        candu = tu | lax.shift_left(jnp.int32(1), 31 - bi)
        cand = candu ^ INT_MIN

        def count_body(j, cnt):
            for half in range(DSA_KT // QB):
                cnt = cnt + jnp.where(key_s[j, half * QB:(half + 1) * QB, :] >= cand, 1.0, 0.0)
            return cnt

        cnt = lax.fori_loop(0, n_t, count_body, jnp.zeros((QB, QB), F32))
        return jnp.where(jnp.sum(cnt, axis=0, keepdims=True) >= topk, candu, tu)

    thr = lax.fori_loop(0, 32, bit_body, jnp.zeros((1, QB), I32)) ^ INT_MIN

    ot_s[...] = jnp.zeros(ot_s.shape, F32)
    scale = HEAD_DIM ** -0.5
    rep = H_A // KV_A

    def logits(j):
        sel = (key_s[j] >= thr) & causal(j)
        out = []
        for g in range(KV_A):
            kt = k_ref[key_rows(j), g * HEAD_DIM:(g + 1) * HEAD_DIM].astype(BF16)
            for r in range(rep):
                h = g * rep + r
                out.append(_dot_nt(kt, qb_s[:, h * HEAD_DIM:(h + 1) * HEAD_DIM]) * scale)
        return sel, out

    def max_body(j, ms):
        sel, sts = logits(j)
        return tuple(jnp.maximum(m, jnp.max(jnp.where(sel, st, NEG), axis=0, keepdims=True))
                     for m, st in zip(ms, sts))

    ms = lax.fori_loop(0, n_t, max_body, tuple(jnp.full((1, QB), NEG, F32) for _ in range(H_A)))

    def att_body(j, ls):
        sel, sts = logits(j)
        out = []
        for h in range(H_A):
            p = jnp.where(sel, jnp.exp(sts[h] - ms[h]), 0.0)
            out.append(ls[h] + jnp.sum(p, axis=0, keepdims=True))
            ot_s[h] += _dot(vt_s[h // rep, j], p.astype(BF16))
        return tuple(out)

    ls = lax.fori_loop(0, n_t, att_body, tuple(jnp.zeros((1, QB), F32) for _ in range(H_A)))

    for h in range(H_A):
        o_ref[:, h * HEAD_DIM:(h + 1) * HEAD_DIM] = (ot_s[h] / ls[h]).T.astype(o_ref.dtype)


def _dsa_prompt(hp, off, batch, seq):
    nq = seq // QB
    topk = min(TOPK_MAX, seq // 4)
    kvw = KV_A * HEAD_DIM
    iqw = IDX_H * IDX_DIM
    return pl.pallas_call(
        functools.partial(_dsa_prompt_kernel, seq=seq, topk=topk),
        grid=(batch, nq),
        in_specs=[pl.BlockSpec((QB, A_W), lambda b, i: (b * nq + i, off["q_a"] // A_W)),
                  pl.BlockSpec((QB, iqw), lambda b, i: (b * nq + i, off["iq"] // iqw)),
                  pl.BlockSpec((QB, LANES), lambda b, i: (b * nq + i, off["iw"] // LANES)),
                  pl.BlockSpec((seq, kvw), lambda b, i: (b, off["k_a"] // kvw)),
                  pl.BlockSpec((seq, kvw), lambda b, i: (b, off["v_a"] // kvw)),
                  pl.BlockSpec((seq, IDX_DIM), lambda b, i: (b, off["ik"] // IDX_DIM))],
        out_specs=pl.BlockSpec((QB, A_W), lambda b, i: (b * nq + i, 0)),
        out_shape=jax.ShapeDtypeStruct((batch * seq, A_W), BF16),
        scratch_shapes=[pltpu.VMEM((IDX_DIM, IDX_H * QB), BF16),
                        pltpu.VMEM((seq // DSA_KT, DSA_KT, QB), I32),
                        pltpu.VMEM((KV_A, seq // DSA_KT, HEAD_DIM, DSA_KT), BF16),
                        pltpu.VMEM((QB, A_W), BF16),
                        pltpu.VMEM((H_A, HEAD_DIM, QB), F32)],
        compiler_params=_cp(("arbitrary", "arbitrary")),
        name="dsa_prompt",
    )(hp, hp, hp, hp, hp, hp)


def _stick_prompt_kernel(q_ref, k_ref, v_ref, o_ref):
    i = pl.program_id(2)
    u2 = _suffix_matrix()
    row = lax.broadcasted_iota(I32, (QB, QB), 0)
    col = lax.broadcasted_iota(I32, (QB, QB), 1)
    scale = HEAD_DIM ** -0.5
    heads = [slice(h * HEAD_DIM, (h + 1) * HEAD_DIM) for h in range(STICK_HG)]
    qs = [q_ref[:, hs].astype(BF16) for hs in heads]

    def cond(c):
        return (c[0] <= i) & (c[1] == 0)

    def body(c):
        jj, _, accs, tots = c
        j = i - jj
        rows = pl.ds(pl.multiple_of(j * QB, QB), QB)
        mask = (j * QB + col) < (i * QB + row)
        new_accs, new_tots = [], []
        for q, hs, acc, tot in zip(qs, heads, accs, tots):
            z = _dot_nt(q, k_ref[rows, hs].astype(BF16)) * scale
            lk, ls = _log_sig_pair(z)
            lk = jnp.where(mask, lk, 0.0)
            within, total = _suffix_sums(lk, u2)
            a = jnp.where(mask, jnp.exp(ls + within + tot), 0.0)
            new_accs.append(acc + _dot(a.astype(BF16), v_ref[rows, hs].astype(BF16)))
            new_tots.append(tot + total)
        worst = functools.reduce(jnp.maximum, new_tots)
        done = (jnp.max(worst) < STICK_EXIT).astype(I32)
        return jj + 1, done, tuple(new_accs), tuple(new_tots)

    zeros = tuple(jnp.zeros((QB, QB), F32) for _ in range(STICK_HG))
    _, _, accs, _ = lax.while_loop(cond, body, (jnp.int32(0), jnp.int32(0), zeros, zeros))
    for hs, acc in zip(heads, accs):
        o_ref[:, hs] = acc.astype(o_ref.dtype)


def _stick_prompt(hp, off, batch, seq):
    nq = seq // QB
    w = STICK_HG * HEAD_DIM
    assert all(off[n] % w == 0 for n in ("q_c", "k_c", "v_c"))
    return pl.pallas_call(
        _stick_prompt_kernel,
        grid=(batch, H_C // STICK_HG, nq),
        in_specs=[pl.BlockSpec((QB, w), lambda b, h, i: (b * nq + i, off["q_c"] // w + h)),
                  pl.BlockSpec((seq, w), lambda b, h, i: (b, off["k_c"] // w + h)),
                  pl.BlockSpec((seq, w), lambda b, h, i: (b, off["v_c"] // w + h))],
        out_specs=pl.BlockSpec((QB, w), lambda b, h, i: (b * nq + i, h)),
        out_shape=jax.ShapeDtypeStruct((batch * seq, C_W), BF16),
        compiler_params=_cp(("arbitrary", "arbitrary", "arbitrary")),
        name="stick_prompt",
    )(hp, hp, hp)


def _gmlp_kernel(u_ref, v_ref, ws_ref, bst_ref, g_ref, b_ref, o_ref, vn_ref):
    vn = _layer_norm(_gelu(v_ref[...]), g_ref[...], b_ref[...])
    vn_ref[...] = vn
    gu = _gelu(u_ref[...])
    row = lax.broadcasted_iota(I32, (CHUNK, CHUNK), 0)
    col = lax.broadcasted_iota(I32, (CHUNK, CHUNK), 1)
    gw = W_B // G_B
    for g in range(G_B):
        w = jnp.where(row >= col, ws_ref[g], 0.0).astype(BF16)
        mixed = _dot(w, vn[:, g * gw:(g + 1) * gw].astype(BF16)) + bst_ref[:, g:g + 1]
        o_ref[:, g * gw:(g + 1) * gw] = (gu[:, g * gw:(g + 1) * gw] * mixed).astype(o_ref.dtype)


def _gmlp(u_src, v_src, u_blk, v_blk, n_chunks, w_s, b_s_t, g, b):
    return pl.pallas_call(
        _gmlp_kernel,
        grid=(n_chunks,),
        in_specs=[pl.BlockSpec((CHUNK, W_B), lambda c: (c, u_blk)),
                  pl.BlockSpec((CHUNK, W_B), lambda c: (c, v_blk)),
                  pl.BlockSpec((G_B, CHUNK, CHUNK), lambda c: (0, 0, 0)),
                  pl.BlockSpec((CHUNK, G_B), lambda c: (0, 0)),
                  pl.BlockSpec((1, W_B), lambda c: (0, 0)),
                  pl.BlockSpec((1, W_B), lambda c: (0, 0))],
        out_specs=[pl.BlockSpec((CHUNK, W_B), lambda c: (c, 0)),
                   pl.BlockSpec((CHUNK, W_B), lambda c: (c, 0))],
        out_shape=[jax.ShapeDtypeStruct((n_chunks * CHUNK, W_B), BF16),
                   jax.ShapeDtypeStruct((n_chunks * CHUNK, W_B), F32)],
        compiler_params=_cp(("arbitrary",)),
        name="gmlp",
    )(u_src, v_src, w_s, b_s_t, g, b)


def _page_specs(block, n_pages, layer, reverse=False):
    def spec(g):
        def index(b, p, pt):
            k = p * PAGE_GROUP + g
            k = n_pages - 1 - k if reverse else k
            return (layer, pt[b * n_pages + k]) + (0,) * (len(block) - 2)
        return pl.BlockSpec(block, index)
    return [spec(g) for g in range(PAGE_GROUP)]


def _dsa_sample_score_kernel(pt_ref, iq_ref, wrow_ref, iknew_ref, *rest, n_groups, n_new, topk):
    del pt_ref
    ik_refs = rest[:PAGE_GROUP]
    i_ref, thr_ref, q2_s, key_s = rest[PAGE_GROUP:]
    p = pl.program_id(1)
    gw = PAGE_GROUP * LANES

    @pl.when(p == 0)
    def _():
        q2 = jnp.concatenate([iq_ref[:, h * IDX_DIM:(h + 1) * IDX_DIM] for h in range(IDX_H)], axis=0)
        q2_s[...] = q2.astype(BF16)

    wrow = wrow_ref[0:1, :] * ((IDX_DIM ** -0.5) * (IDX_H ** -0.5))
    r16 = lax.broadcasted_iota(I32, (2 * n_new, LANES), 0)
    l16 = lax.broadcasted_iota(I32, (2 * n_new, LANES), 1)
    head_sum = jnp.where(l16 % n_new == r16, 1.0, 0.0).astype(BF16)

    def scores(keys):
        r = jnp.maximum(_dot_nt(keys.astype(BF16), q2_s[...]), 0.0) * wrow
        hi = r.astype(BF16)
        lo = (r - hi.astype(F32)).astype(BF16)
        return (_dot_nt(head_sum, hi) + _dot_nt(head_sum, lo))[0:n_new, :]

    it = scores(jnp.concatenate([ref[...] for ref in ik_refs], axis=0))
    i_ref[p] = it
    key_s[p] = _sortable(it)

    @pl.when(p == n_groups - 1)
    def _():
        pad = jnp.zeros((LANES - n_new, IDX_DIM), F32)
        itn = scores(jnp.concatenate([iknew_ref[...], pad], axis=0))
        t = lax.broadcasted_iota(I32, (n_new, LANES), 0)
        s = lax.broadcasted_iota(I32, (n_new, LANES), 1)
        itn = jnp.where((s <= t) & (s < n_new), itn, -jnp.inf)
        itn = jnp.concatenate([itn, jnp.full((n_new, gw - LANES), -jnp.inf, F32)], axis=1)
        i_ref[n_groups] = itn
        key_s[n_groups] = _sortable(itn)

        def bit_body(bi, tu):
            candu = tu | lax.shift_left(jnp.int32(1), 31 - bi)
            cand = candu ^ INT_MIN

            def count_body(j, cnt):
                return cnt + jnp.where(key_s[j] >= cand, 1.0, 0.0)

            cnt = lax.fori_loop(0, n_groups + 1, count_body, jnp.zeros((n_new, gw), F32))
            return jnp.where(jnp.sum(cnt, axis=1, keepdims=True) >= topk, candu, tu)

        thr = lax.fori_loop(0, 32, bit_body, jnp.zeros((n_new, 1), I32)) ^ INT_MIN
        thr_ref[...] = jnp.broadcast_to(thr, thr_ref.shape)


def _dsa_sample_scores(hp, wrow, cache_idx_k, pt_flat, off, layer, mp, dec_batch, n_new, n_pages):
    topk = min(TOPK_MAX, (n_pages * LANES + n_new) // 4)
    iqw = IDX_H * IDX_DIM
    rb = mp // n_new
    n_groups = n_pages // PAGE_GROUP
    gw = PAGE_GROUP * LANES
    grid_spec = pltpu.PrefetchScalarGridSpec(
        num_scalar_prefetch=1,
        grid=(dec_batch, n_groups),
        in_specs=[pl.BlockSpec((n_new, iqw), lambda b, p, pt: (rb + b, off["iq"] // iqw)),
                  pl.BlockSpec((n_new, LANES), lambda b, p, pt: (b, 0)),
                  pl.BlockSpec((n_new, IDX_DIM), lambda b, p, pt: (rb + b, off["ik"] // IDX_DIM))]
        + _page_specs((None, None, LANES, IDX_DIM), n_pages, layer),
        out_specs=[pl.BlockSpec((None, n_groups + 1, n_new, gw), lambda b, p, pt: (b, 0, 0, 0)),
                   pl.BlockSpec((n_new, LANES), lambda b, p, pt: (b, 0))],
        scratch_shapes=[pltpu.VMEM((IDX_H * n_new, IDX_DIM), BF16),
                        pltpu.VMEM((n_groups + 1, n_new, gw), I32)])
    return pl.pallas_call(
        functools.partial(_dsa_sample_score_kernel, n_groups=n_groups, n_new=n_new, topk=topk),
        grid_spec=grid_spec,
        out_shape=[jax.ShapeDtypeStruct((dec_batch, n_groups + 1, n_new, gw), F32),
                   jax.ShapeDtypeStruct((dec_batch * n_new, LANES), I32)],
        compiler_params=_cp(("arbitrary", "arbitrary")),
        name="dsa_sample_scores",
    )(pt_flat, hp, wrow, hp, *([cache_idx_k] * PAGE_GROUP))


def _dsa_sample_attend_kernel(pt_ref, q_ref, i_ref, inew_ref, thr_ref, knew_ref, vnew_ref, *rest, n_groups, n_new):
    del pt_ref
    k_refs, v_refs = rest[:PAGE_GROUP], rest[PAGE_GROUP:2 * PAGE_GROUP]
    o_ref, qbd_s, o_s, m_s, l_s = rest[2 * PAGE_GROUP:]
    p = pl.program_id(1)
    rep = H_A // KV_A
    n_rows = H_A * n_new
    grp = lax.broadcasted_iota(I32, (n_rows, 1), 0) // (rep * n_new)
    scale = HEAD_DIM ** -0.5

    @pl.when(p == 0)
    def _():
        qs = jnp.concatenate([q_ref[:, h * HEAD_DIM:(h + 1) * HEAD_DIM] for h in range(H_A)], axis=0)
        qbd_s[...] = jnp.concatenate([jnp.where(grp == g, qs, 0.0) for g in range(KV_A)], axis=1).astype(BF16)
        m_s[...] = jnp.full(m_s.shape, NEG, F32)
        l_s[...] = jnp.zeros(l_s.shape, F32)
        o_s[...] = jnp.zeros(o_s.shape, F32)

    def update(sel_q, kflat, vstack):
        sel = jnp.concatenate([jnp.where(sel_q, 1.0, 0.0)] * H_A, axis=0) > 0.5
        s = _dot_nt(qbd_s[...], kflat.astype(BF16)) * scale
        m_old = m_s[...]
        m_new = jnp.maximum(m_old, jnp.max(jnp.where(sel, s, NEG), axis=1, keepdims=True))
        pr = jnp.where(sel, jnp.exp(s - m_new), 0.0)
        alpha = jnp.exp(m_old - m_new)
        l_s[...] = alpha * l_s[...] + jnp.sum(pr, axis=1, keepdims=True)
        pbd = jnp.concatenate([jnp.where(grp == g, pr, 0.0) for g in range(KV_A)], axis=1).astype(BF16)
        o_s[...] = alpha * o_s[...] + _dot(pbd, vstack.astype(BF16))
        m_s[...] = m_new

    thr = thr_ref[:, 0:1]
    kflat = jnp.concatenate([jnp.concatenate([k[:, g, :] for g in range(KV_A)], axis=1) for k in k_refs], axis=0)
    vstack = jnp.concatenate([v[:, g, :] for g in range(KV_A) for v in v_refs], axis=0)
    update(_sortable(i_ref[...]) >= thr, kflat, vstack)

    @pl.when(p == n_groups - 1)
    def _():
        t = lax.broadcasted_iota(I32, (n_new, LANES), 0)
        s = lax.broadcasted_iota(I32, (n_new, LANES), 1)
        sel_new = (_sortable(inew_ref[:, 0:LANES]) >= thr) & (s <= t) & (s < n_new)
        pad = jnp.zeros((LANES - n_new, KV_A * HEAD_DIM), F32)
        kn = jnp.concatenate([knew_ref[...], pad], axis=0)
        vn = jnp.concatenate([vnew_ref[...], pad], axis=0)
        vns = jnp.concatenate([vn[:, g * HEAD_DIM:(g + 1) * HEAD_DIM] for g in range(KV_A)], axis=0)
        update(sel_new, kn, vns)
        o = o_s[...] / l_s[...]
        for h in range(H_A):
            o_ref[:, h * HEAD_DIM:(h + 1) * HEAD_DIM] = o[h * n_new:(h + 1) * n_new, :]


def _dsa_sample_attend(hp, scores, thr, cache_k_a, cache_v_a, pt_flat, off, layer, mp, dec_batch, n_new, n_pages):
    kvw = KV_A * HEAD_DIM
    rb = mp // n_new
    n_rows = H_A * n_new
    n_groups = n_pages // PAGE_GROUP
    gw = PAGE_GROUP * LANES
    page_block = (None, None, LANES, KV_A, HEAD_DIM)
    grid_spec = pltpu.PrefetchScalarGridSpec(
        num_scalar_prefetch=1,
        grid=(dec_batch, n_groups),
        in_specs=[pl.BlockSpec((n_new, A_W), lambda b, p, pt: (rb + b, off["q_a"] // A_W)),
                  pl.BlockSpec((None, None, n_new, gw), lambda b, p, pt: (b, p, 0, 0)),
                  pl.BlockSpec((None, None, n_new, gw), lambda b, p, pt: (b, n_groups, 0, 0)),
                  pl.BlockSpec((n_new, LANES), lambda b, p, pt: (b, 0)),
                  pl.BlockSpec((n_new, kvw), lambda b, p, pt: (rb + b, off["k_a"] // kvw)),
                  pl.BlockSpec((n_new, kvw), lambda b, p, pt: (rb + b, off["v_a"] // kvw))]
        + _page_specs(page_block, n_pages, layer) + _page_specs(page_block, n_pages, layer),
        out_specs=pl.BlockSpec((n_new, A_W), lambda b, p, pt: (b, 0)),
        scratch_shapes=[pltpu.VMEM((n_rows, KV_A * HEAD_DIM), BF16),
                        pltpu.VMEM((n_rows, HEAD_DIM), F32),
                        pltpu.VMEM((n_rows, 1), F32),
                        pltpu.VMEM((n_rows, 1), F32)])
    return pl.pallas_call(
        functools.partial(_dsa_sample_attend_kernel, n_groups=n_groups, n_new=n_new),
        grid_spec=grid_spec,
        out_shape=jax.ShapeDtypeStruct((dec_batch * n_new, A_W), F32),
        compiler_params=_cp(("arbitrary", "arbitrary")),
        name="dsa_sample_attend",
    )(pt_flat, hp, scores, scores, thr, hp, hp, *([cache_k_a] * PAGE_GROUP), *([cache_v_a] * PAGE_GROUP))


def _stick_sample_kernel(pt_ref, q_ref, knew_ref, vnew_ref, kc_ref, vc_ref, o_ref,
                         kbuf, vbuf, sem, qbd_s, acc_s, tot_s, *, layer, n_pages, n_new):
    b = pl.program_id(0)
    n_groups = n_pages // PAGE_GROUP
    n_rows = H_C * n_new
    head = lax.broadcasted_iota(I32, (n_rows, LANES), 0) // n_new
    u2 = _suffix_matrix()
    scale = HEAD_DIM ** -0.5

    def sweep(mask, kflat, vstack):
        z = _dot_nt(qbd_s[...], kflat.astype(BF16)) * scale
        lk, ls = _log_sig_pair(z)
        if mask is not None:
            lk = jnp.where(mask, lk, 0.0)
        within, total = _suffix_sums(lk, u2)
        a = jnp.exp(ls + within + tot_s[...])
        if mask is not None:
            a = jnp.where(mask, a, 0.0)
        abd = jnp.concatenate([jnp.where(head == h, a, 0.0) for h in range(H_C)], axis=1).astype(BF16)
        acc_s[...] = acc_s[...] + _dot(abd, vstack.astype(BF16))
        tot_s[...] = tot_s[...] + total

    qs = jnp.concatenate([q_ref[:, h * HEAD_DIM:(h + 1) * HEAD_DIM] for h in range(H_C)], axis=0)
    qbd_s[...] = jnp.concatenate([jnp.where(head == h, qs, 0.0) for h in range(H_C)], axis=1).astype(BF16)
    acc_s[...] = jnp.zeros(acc_s.shape, F32)
    tot_s[...] = jnp.zeros(tot_s.shape, F32)
    t = lax.broadcasted_iota(I32, (n_rows, LANES), 0) % n_new
    s = lax.broadcasted_iota(I32, (n_rows, LANES), 1)
    pad = jnp.zeros((LANES - n_new, C_W), F32)
    kn = jnp.concatenate([knew_ref[...], pad], axis=0)
    vn = jnp.concatenate([vnew_ref[...], pad], axis=0)
    vns = jnp.concatenate([vn[:, h * HEAD_DIM:(h + 1) * HEAD_DIM] for h in range(H_C)], axis=0)
    sweep(s < t, kn, vns)

    def page_copies(grp):
        out = []
        for g in range(PAGE_GROUP):
            page = pt_ref[b * n_pages + n_pages - 1 - (grp * PAGE_GROUP + g)]
            out.append(pltpu.make_async_copy(kc_ref.at[layer, page], kbuf.at[g], sem.at[0]))
            out.append(pltpu.make_async_copy(vc_ref.at[layer, page], vbuf.at[g], sem.at[1]))
        return out

    def all_zero_from_here():
        return (jnp.max(tot_s[...]) < STICK_EXIT).astype(I32)

    def cond(c):
        return (c[0] < n_groups) & (c[1] == 0)

    def body(c):
        grp = c[0]
        for cp in page_copies(grp):
            cp.start()
        for cp in page_copies(grp):
            cp.wait()
        for g in range(PAGE_GROUP):
            kflat = jnp.concatenate([kbuf[g, h] for h in range(H_C)], axis=1)
            vstack = jnp.concatenate([vbuf[g, h] for h in range(H_C)], axis=0)
            sweep(None, kflat, vstack)
        return grp + 1, all_zero_from_here()

    lax.while_loop(cond, body, (jnp.int32(0), all_zero_from_here()))

    acc = acc_s[...]
    for h in range(H_C):
        o_ref[:, h * HEAD_DIM:(h + 1) * HEAD_DIM] = acc[h * n_new:(h + 1) * n_new, :]


def _stick_sample(hp, cache_k_c_t, cache_v_c_t, pt_flat, off, layer, mp, dec_batch, n_new, n_pages):
    rb = mp // n_new
    n_rows = H_C * n_new
    page_buf = pltpu.VMEM((PAGE_GROUP, H_C, LANES, HEAD_DIM), F32)
    grid_spec = pltpu.PrefetchScalarGridSpec(
        num_scalar_prefetch=1,
        grid=(dec_batch,),
        in_specs=[pl.BlockSpec((n_new, C_W), lambda b, pt: (rb + b, off["q_c"] // C_W)),
                  pl.BlockSpec((n_new, C_W), lambda b, pt: (rb + b, off["k_c"] // C_W)),
                  pl.BlockSpec((n_new, C_W), lambda b, pt: (rb + b, off["v_c"] // C_W)),
                  pl.BlockSpec(memory_space=pl.ANY),
                  pl.BlockSpec(memory_space=pl.ANY)],
        out_specs=pl.BlockSpec((n_new, C_W), lambda b, pt: (b, 0)),
        scratch_shapes=[page_buf, page_buf, pltpu.SemaphoreType.DMA((2,)),
                        pltpu.VMEM((n_rows, C_W), BF16),
                        pltpu.VMEM((n_rows, HEAD_DIM), F32),
                        pltpu.VMEM((n_rows, LANES), F32)])
    return pl.pallas_call(
        functools.partial(_stick_sample_kernel, layer=layer, n_pages=n_pages, n_new=n_new),
        grid_spec=grid_spec,
        out_shape=jax.ShapeDtypeStruct((dec_batch * n_new, C_W), F32),
        compiler_params=_cp(("arbitrary",)),
        name="stick_sample",
    )(pt_flat, hp, hp, hp, cache_k_c_t, cache_v_c_t)


def _merge_kernel(oa_ref, ob_ref, oc_ref, ga_ref, gb_ref, gc_ref, wpa_ref, wpb_ref, wpc_ref, m_ref):
    m = (jax.nn.sigmoid(ga_ref[...]) * _dot(oa_ref[...], wpa_ref[...])
         + jax.nn.sigmoid(gb_ref[...]) * _dot(ob_ref[...], wpb_ref[...])
         + jax.nn.sigmoid(gc_ref[...]) * _dot(oc_ref[...], wpc_ref[...]))
    m_ref[...] = m.astype(m_ref.dtype)


def _merge(o_a, o_b, o_c, hp, off, w_pa, w_pb, w_pc, layer):
    mt = hp.shape[0]
    d = w_pa.shape[2]
    gate = lambda name: pl.BlockSpec((TM, d), lambda i: (i, off[name] // d))
    wspec = lambda k: pl.BlockSpec((None, k, d), lambda i: (layer, 0, 0))
    return pl.pallas_call(
        _merge_kernel,
        grid=(mt // TM,),
        in_specs=[pl.BlockSpec((TM, A_W), lambda i: (i, 0)),
                  pl.BlockSpec((TM, W_B), lambda i: (i, 0)),
                  pl.BlockSpec((TM, C_W), lambda i: (i, 0)),
                  gate("g_a"), gate("g_b"), gate("g_c"),
                  wspec(A_W), wspec(W_B), wspec(C_W)],
        out_specs=pl.BlockSpec((TM, d), lambda i: (i, 0)),
        out_shape=jax.ShapeDtypeStruct((mt, d), BF16),
        compiler_params=_cp(("arbitrary",)),
        name="merge",
    )(o_a, o_b, o_c, hp, hp, hp, w_pa, w_pb, w_pc)


def _post_mix_kernel(m_ref, x_ref, wout_ref, g_ref, b_ref, wr_ref, br_ref,
                     x1_ref, x1b_ref, x1p_ref, idx_ref, wsel_ref, *, alpha, n_valid, n_experts):
    x1 = _layer_norm(alpha * x_ref[...] + _dot(m_ref[...], wout_ref[...]), g_ref[...], b_ref[...])
    x1_ref[...] = x1
    x1b = x1.astype(BF16)
    x1b_ref[...] = x1b
    x1p_ref[...] = _pack_halves(x1b)
    scores = jax.nn.sigmoid(_dot(x1b, wr_ref[...]))
    lane = lax.broadcasted_iota(I32, scores.shape, 1)
    lane_f = lane.astype(F32)
    sel = jnp.where(lane < n_experts, scores + br_ref[...], -jnp.inf)
    idx = jnp.full(scores.shape, -1.0, F32)
    wts = jnp.zeros(scores.shape, F32)
    for k in range(TOP_K):
        mx = jnp.max(sel, axis=1, keepdims=True)
        ik = jnp.min(jnp.where(sel == mx, lane_f, float(LANES)), axis=1, keepdims=True)
        hit = lane_f == ik
        wk = jnp.sum(jnp.where(hit, scores, 0.0), axis=1, keepdims=True)
        sel = jnp.where(hit, -jnp.inf, sel)
        idx = jnp.where(lane == k, ik, idx)
        wts = jnp.where(lane == k, wk, wts)
    wts = wts / jnp.sum(wts, axis=1, keepdims=True) * ROUTED_SCALE
    rowid = pl.program_id(0) * TM + lax.broadcasted_iota(I32, scores.shape, 0)
    valid = rowid < n_valid
    idx_ref[...] = jnp.where(valid, idx, -1.0).astype(I32)
    wsel_ref[...] = jnp.where(valid, wts, 0.0)


def _post_mix(m, xf, w_out, g, b, w_r, b_r, layer, alpha, n_valid, n_experts):
    mt, d = xf.shape
    row = lambda i: (i, 0)
    vec = lambda n: pl.BlockSpec((None, 1, n), lambda i: (layer, 0, 0))
    return pl.pallas_call(
        functools.partial(_post_mix_kernel, alpha=alpha, n_valid=n_valid, n_experts=n_experts),
        grid=(mt // TM,),
        in_specs=[pl.BlockSpec((TM, d), row), pl.BlockSpec((TM, d), row),
                  pl.BlockSpec((None, d, d), lambda i: (layer, 0, 0)),
                  vec(d), vec(d),
                  pl.BlockSpec((None, d, LANES), lambda i: (layer, 0, 0)),
                  vec(LANES)],
        out_specs=[pl.BlockSpec((TM, d), row), pl.BlockSpec((TM, d), row), pl.BlockSpec((TM, d // 2), row),
                   pl.BlockSpec((TM, LANES), row), pl.BlockSpec((TM, LANES), row)],
        out_shape=[jax.ShapeDtypeStruct((mt, d), F32), jax.ShapeDtypeStruct((mt, d), BF16),
                   jax.ShapeDtypeStruct((mt, d // 2), I32),
                   jax.ShapeDtypeStruct((mt, LANES), I32), jax.ShapeDtypeStruct((mt, LANES), F32)],
        compiler_params=_cp(("arbitrary",)),
        name="post_mix",
    )(m, xf, w_out, g, b, w_r, b_r)


def _route_kernel(idx_ref, pos_ref, be_ref, nu_ref, *, mt, n_experts, nbp):
    n_tiles = mt // ROUTE_TR
    lane = lax.broadcasted_iota(I32, (ROUTE_TR, LANES), 1)

    def one_hot(t):
        idx = idx_ref[pl.ds(pl.multiple_of(t * ROUTE_TR, ROUTE_TR), ROUTE_TR), :]
        oh = jnp.zeros((ROUTE_TR, LANES), F32)
        for k in range(TOP_K):
            oh = oh + jnp.where(idx[:, k:k + 1] == lane, 1.0, 0.0)
        return idx, oh

    def count_body(t, cnt):
        return cnt + jnp.sum(one_hot(t)[1], axis=0, keepdims=True)

    cnt = lax.fori_loop(0, n_tiles, count_body, jnp.zeros((1, LANES), F32))
    nb = jnp.floor((cnt + (EXPERT_BLK - 1)) * (1.0 / EXPERT_BLK))
    r = lax.broadcasted_iota(I32, (LANES, LANES), 0)
    c = lax.broadcasted_iota(I32, (LANES, LANES), 1)
    nb8 = jnp.broadcast_to(nb, (8, LANES))
    start = _dot(nb8.astype(BF16), jnp.where(r < c, 1.0, 0.0).astype(BF16))
    pend = start + nb8
    start_row = start[0:1, :] * float(EXPERT_BLK)

    rr = lax.broadcasted_iota(I32, (ROUTE_TR, ROUTE_TR), 0)
    cc = lax.broadcasted_iota(I32, (ROUTE_TR, ROUTE_TR), 1)
    lower = jnp.where(cc < rr, 1.0, 0.0).astype(BF16)

    def pos_body(t, seen):
        idx, oh = one_hot(t)
        pe = start_row + seen + _dot(lower, oh.astype(BF16))
        pos = jnp.zeros((ROUTE_TR, LANES), F32)
        for k in range(TOP_K):
            pk = jnp.sum(jnp.where(idx[:, k:k + 1] == lane, pe, 0.0), axis=1, keepdims=True)
            pos = jnp.where(lane == k, pk, pos)
        pos_ref[pl.ds(pl.multiple_of(t * ROUTE_TR, ROUTE_TR), ROUTE_TR), :] = pos.astype(I32)
        return seen + jnp.sum(oh, axis=0, keepdims=True)

    lax.fori_loop(0, n_tiles, pos_body, jnp.zeros((1, LANES), F32))

    pend_t = jnp.broadcast_to(pend[0:1, :], (LANES, LANES)).T
    for ch in range(nbp // LANES):
        blk = (ch * LANES + c).astype(F32)
        be = jnp.sum(jnp.where((pend_t <= blk) & (r < n_experts), 1.0, 0.0), axis=0, keepdims=True)
        be = jnp.minimum(be, float(n_experts - 1))
        be_ref[:, ch * LANES:(ch + 1) * LANES] = jnp.broadcast_to(be, (8, LANES)).astype(I32)
    nu_ref[...] = jnp.broadcast_to(jnp.max(pend, axis=1, keepdims=True), (8, LANES)).astype(I32)


def _route(idx, n_experts, n_blocks):
    mt = idx.shape[0]
    nbp = -(-n_blocks // LANES) * LANES
    return pl.pallas_call(
        functools.partial(_route_kernel, mt=mt, n_experts=n_experts, nbp=nbp),
        out_shape=[jax.ShapeDtypeStruct((mt, LANES), I32),
                   jax.ShapeDtypeStruct((8, nbp), I32),
                   jax.ShapeDtypeStruct((8, LANES), I32)],
        compiler_params=pltpu.CompilerParams(vmem_limit_bytes=VMEM_LIMIT),
        name="route",
    )(idx)


def _dispatch_kernel(pos_ref, x_ref, xs_in_ref, xs_ref, sem, *, n_valid):
    del xs_in_ref
    t0 = pl.program_id(0) * DISPATCH_TD
    n = jnp.clip(n_valid - t0, 0, DISPATCH_TD)

    def copy(r, k):
        dst = pos_ref[(t0 + r) * TOP_K + k]
        return pltpu.make_async_copy(x_ref.at[pl.ds(r, 1), :], xs_ref.at[pl.ds(dst, 1), :], sem)

    def start_body(r, c):
        for k in range(TOP_K):
            copy(r, k).start()
        return c

    def wait_body(r, c):
        for k in range(TOP_K):
            copy(r, k).wait()
        return c

    @pl.when(n == DISPATCH_TD)
    def _():
        lax.fori_loop(0, DISPATCH_TD, start_body, 0, unroll=True)
        for _ in range(TOP_K):
            pltpu.make_async_copy(x_ref, xs_ref.at[pl.ds(0, DISPATCH_TD), :], sem).wait()

    @pl.when(n != DISPATCH_TD)
    def _():
        lax.fori_loop(0, n, start_body, 0)
        lax.fori_loop(0, n, wait_body, 0)


def _dispatch(pos_flat, x1p, n_rows, n_valid):
    mt, d = x1p.shape
    grid_spec = pltpu.PrefetchScalarGridSpec(
        num_scalar_prefetch=1,
        grid=(mt // DISPATCH_TD,),
        in_specs=[pl.BlockSpec((DISPATCH_TD, d), lambda i, pos: (i, 0)),
                  pl.BlockSpec(memory_space=pl.ANY)],
        out_specs=pl.BlockSpec(memory_space=pl.ANY),
        scratch_shapes=[pltpu.SemaphoreType.DMA(())])
    return pl.pallas_call(
        functools.partial(_dispatch_kernel, n_valid=n_valid),
        grid_spec=grid_spec,
        out_shape=jax.ShapeDtypeStruct((n_rows, d), I32),
        input_output_aliases={2: 0},
        compiler_params=_cp(("arbitrary",)),
        name="dispatch",
    )(pos_flat, x1p, jnp.zeros((n_rows, d), I32))


def _expert_kernel(be_ref, nu_ref, x_ref, wg_ref, wu_ref, wd_ref, y_ref, wg_s, wu_s, wd_s):
    i = pl.program_id(0)
    used = i < nu_ref[0]
    new_expert = (i == 0) | (be_ref[i] != be_ref[jnp.maximum(i - 1, 0)])

    @pl.when(used & new_expert)
    def _():
        wg_s[...] = wg_ref[...].astype(BF16)
        wu_s[...] = wu_ref[...].astype(BF16)
        wd_s[...] = wd_ref[...].astype(BF16)

    @pl.when(used)
    def _():
        lo, hi = _unpack_halves(x_ref[...])
        half = lo.shape[1]
        gate = _dot(lo, wg_s[:half, :]) + _dot(hi, wg_s[half:, :])
        up = _dot(lo, wu_s[:half, :]) + _dot(hi, wu_s[half:, :])
        y_ref[...] = _dot((jax.nn.silu(gate) * up).astype(BF16), wd_s[...])

    @pl.when(jnp.logical_not(used))
    def _():
        y_ref[...] = jnp.zeros(y_ref.shape, F32)


def _experts(xs, be, nu, w_gate, w_up, w_down, layer, n_blocks):
    n_rows = xs.shape[0]
    d, f = w_gate.shape[2], w_gate.shape[3]
    blk = lambda i, be, nu: (jnp.minimum(i, nu[0] - 1), 0)
    grid_spec = pltpu.PrefetchScalarGridSpec(
        num_scalar_prefetch=2,
        grid=(n_blocks,),
        in_specs=[pl.BlockSpec((EXPERT_BLK, d // 2), blk),
                  pl.BlockSpec((None, None, d, f), lambda i, be, nu: (layer, be[i], 0, 0)),
                  pl.BlockSpec((None, None, d, f), lambda i, be, nu: (layer, be[i], 0, 0)),
                  pl.BlockSpec((None, None, f, d), lambda i, be, nu: (layer, be[i], 0, 0))],
        out_specs=pl.BlockSpec((EXPERT_BLK, d), lambda i, be, nu: (i, 0)),
        scratch_shapes=[pltpu.VMEM((d, f), BF16), pltpu.VMEM((d, f), BF16), pltpu.VMEM((f, d), BF16)])
    return pl.pallas_call(
        _expert_kernel,
        grid_spec=grid_spec,
        out_shape=jax.ShapeDtypeStruct((n_rows, d), F32),
        compiler_params=_cp(("arbitrary",)),
        name="experts",
    )(be, nu, xs, w_gate, w_up, w_down)


def _combine_kernel(pos_ref, ys_ref, wsel_ref, x1_ref, x1b_ref, wsg_ref, wsu_ref, wsd_ref, g_ref, b_ref,
                    x2_ref, x2b_ref, buf, sem, *, alpha, n_steps):
    i = pl.program_id(0)
    slot = i % 2

    def gather(step, s):
        t0 = step * COMBINE_TC

        def body(r, c):
            for k in range(TOP_K):
                src = pos_ref[(t0 + r) * TOP_K + k]
                pltpu.make_async_copy(ys_ref.at[pl.ds(src, 1), :],
                                      buf.at[s, pl.ds(k * COMBINE_TC + r, 1), :], sem.at[s]).start()
            return c

        lax.fori_loop(0, COMBINE_TC, body, 0, unroll=True)

    def wait_slot(s):
        pltpu.make_async_copy(ys_ref.at[pl.ds(0, TOP_K * COMBINE_TC), :], buf.at[s], sem.at[s]).wait()

    @pl.when(i == 0)
    def _():
        gather(0, 0)

    gather(jnp.minimum(i + 1, n_steps - 1), 1 - slot)
    x1b = x1b_ref[...]
    shared = _dot((jax.nn.silu(_dot(x1b, wsg_ref[...])) * _dot(x1b, wsu_ref[...])).astype(BF16), wsd_ref[...])
    wait_slot(slot)

    @pl.when(i == n_steps - 1)
    def _():
        wait_slot(1 - slot)

    wsel = wsel_ref[...]
    routed = wsel[:, 0:1] * buf[slot, 0:COMBINE_TC, :]
    for k in range(1, TOP_K):
        routed = routed + wsel[:, k:k + 1] * buf[slot, k * COMBINE_TC:(k + 1) * COMBINE_TC, :]
    x2 = _layer_norm(alpha * x1_ref[...] + (routed + shared), g_ref[...], b_ref[...])
    x2_ref[...] = x2
    x2b_ref[...] = x2.astype(BF16)


def _combine(pos_flat, ys, wsel, x1, x1b, ws_gate, ws_up, ws_down, g, b, layer, alpha):
    mt, d = x1.shape
    f = ws_gate.shape[2]
    row = lambda i, pos: (i, 0)
    vec = pl.BlockSpec((None, 1, d), lambda i, pos: (layer, 0, 0))
    grid_spec = pltpu.PrefetchScalarGridSpec(
        num_scalar_prefetch=1,
        grid=(mt // COMBINE_TC,),
        in_specs=[pl.BlockSpec(memory_space=pl.ANY),
                  pl.BlockSpec((COMBINE_TC, LANES), row),
                  pl.BlockSpec((COMBINE_TC, d), row),
                  pl.BlockSpec((COMBINE_TC, d), row),
                  pl.BlockSpec((None, d, f), lambda i, pos: (layer, 0, 0)),
                  pl.BlockSpec((None, d, f), lambda i, pos: (layer, 0, 0)),
                  pl.BlockSpec((None, f, d), lambda i, pos: (layer, 0, 0)),
                  vec, vec],
        out_specs=[pl.BlockSpec((COMBINE_TC, d), row), pl.BlockSpec((COMBINE_TC, d), row)],
        scratch_shapes=[pltpu.VMEM((2, TOP_K * COMBINE_TC, d), F32), pltpu.SemaphoreType.DMA((2,))])
    return pl.pallas_call(
        functools.partial(_combine_kernel, alpha=alpha, n_steps=mt // COMBINE_TC),
        grid_spec=grid_spec,
        out_shape=[jax.ShapeDtypeStruct((mt, d), F32), jax.ShapeDtypeStruct((mt, d), BF16)],
        compiler_params=_cp(("arbitrary",)),
        name="combine",
    )(pos_flat, ys, wsel, x1, x1b, ws_gate, ws_up, ws_down, g, b)


def kernel(x_prompt, x_sample, cache_k_a, cache_v_a, cache_idx_k, cache_k_c, cache_v_c, page_table, w_in, w_s, b_s, ln_v_g, ln_v_b, w_pa, w_pb, w_pc, w_out, ln1_g, ln1_b, w_router, b_router, w_gate, w_up, w_down, ws_gate, ws_up, ws_down, ln2_g, ln2_b):
    batch, seq, d = x_prompt.shape
    dec_batch, n_new, _ = x_sample.shape
    depth = w_in.shape[0]
    n_pages = page_table.shape[1]
    n_experts = w_router.shape[2]
    alpha = float((2 * depth) ** 0.25)
    mp, ms = batch * seq, dec_batch * n_new
    n_valid = mp + ms
    mt = -(-n_valid // TM) * TM
    assert seq % QB == 0 and n_new == 8 and n_experts <= LANES and mp % TM == 0
    n_blocks = -(-(n_valid * TOP_K + n_experts * (EXPERT_BLK - 1)) // EXPERT_BLK)
    n_rows = n_blocks * EXPERT_BLK

    pieces, src, off, n_cols = _layout(d)
    cols = []
    for name, w in pieces:
        s0, sw = src[name]
        piece = w_in[:, :, s0:s0 + sw]
        if sw < w:
            piece = jnp.pad(piece, ((0, 0), (0, 0), (0, w - sw)))
        cols.append(piece)
    w_in_r = jnp.concatenate(cols, axis=2).astype(BF16)

    bf = lambda a: a.astype(BF16)
    w_pa_b, w_pb_b, w_pc_b, w_out_b = bf(w_pa), bf(w_pb), bf(w_pc), bf(w_out)
    ws_gate_b, ws_up_b, ws_down_b = bf(ws_gate), bf(ws_up), bf(ws_down)
    w_r_b = bf(jnp.pad(w_router, ((0, 0), (0, 0), (0, LANES - n_experts))))
    b_r = jnp.pad(b_router, ((0, 0), (0, LANES - n_experts)))[:, None, :]
    b_s_t = jnp.swapaxes(b_s, 1, 2)
    vec3 = lambda a: a[:, None, :]
    ln1_g3, ln1_b3, ln2_g3, ln2_b3 = vec3(ln1_g), vec3(ln1_b), vec3(ln2_g), vec3(ln2_b)
    pt_flat = page_table.reshape(-1).astype(I32)
    assert n_pages % PAGE_GROUP == 0
    cache_k_c_t = jnp.swapaxes(cache_k_c, 2, 3)
    cache_v_c_t = jnp.swapaxes(cache_v_c, 2, 3)

    xf = jnp.concatenate([x_prompt.reshape(mp, d), x_sample.reshape(ms, d),
                          jnp.zeros((mt - n_valid, d), F32)], axis=0)
    xb = xf.astype(BF16)

    st = [[] for _ in range(11)]
    for l in range(depth):
        hp = _in_proj(xb, w_in_r, l)

        def cut(name, lo, hi, width):
            return hp[lo:hi, off[name]:off[name] + width]

        o_a_p = _dsa_prompt(hp, off, batch, seq)
        o_c_p = _stick_prompt(hp, off, batch, seq)
        o_b_p, _ = _gmlp(hp, hp, off["u_b"] // W_B, off["v_b"] // W_B, mp // CHUNK,
                         w_s[l], b_s_t[l], ln_v_g[l][None], ln_v_b[l][None])

        iw_s = cut("iw", mp, mp + ms, IDX_H).reshape(dec_batch, n_new, IDX_H)
        wrow = jnp.swapaxes(iw_s, 1, 2).reshape(dec_batch, 1, IDX_H * n_new)
        wrow = jnp.broadcast_to(wrow, (dec_batch, n_new, IDX_H * n_new)).reshape(ms, IDX_H * n_new)
        scores, thr = _dsa_sample_scores(hp, wrow, cache_idx_k, pt_flat, off, l, mp, dec_batch, n_new, n_pages)
        o_a_s = _dsa_sample_attend(hp, scores, thr, cache_k_a, cache_v_a, pt_flat, off, l, mp,
                                   dec_batch, n_new, n_pages)
        o_c_s = _stick_sample(hp, cache_k_c_t, cache_v_c_t, pt_flat, off, l, mp, dec_batch, n_new, n_pages)
        pad_chunk = lambda a: jnp.pad(a.reshape(dec_batch, n_new, W_B),
                                      ((0, 0), (0, CHUNK - n_new), (0, 0))).reshape(dec_batch * CHUNK, W_B)
        o_b_s, vn_s = _gmlp(pad_chunk(cut("u_b", mp, mp + ms, W_B)), pad_chunk(cut("v_b", mp, mp + ms, W_B)),
                            0, 0, dec_batch, w_s[l], b_s_t[l], ln_v_g[l][None], ln_v_b[l][None])
        unpad = lambda a: a.reshape(dec_batch, CHUNK, W_B)[:, :n_new].reshape(ms, W_B)

        tail = lambda w, dt: jnp.zeros((mt - n_valid, w), dt)
        o_a = jnp.concatenate([o_a_p, o_a_s.astype(BF16), tail(A_W, BF16)], axis=0)
        o_b = jnp.concatenate([o_b_p, unpad(o_b_s), tail(W_B, BF16)], axis=0)
        o_c = jnp.concatenate([o_c_p, o_c_s.astype(BF16), tail(C_W, BF16)], axis=0)

        m = _merge(o_a, o_b, o_c, hp, off, w_pa_b, w_pb_b, w_pc_b, l)
        x1, x1b, x1p, idx, wsel = _post_mix(m, xf, w_out_b, ln1_g3, ln1_b3, w_r_b, b_r, l, alpha, n_valid,
                                            n_experts)

        pos, be, nu = _route(idx, n_experts, n_blocks)
        pos_flat = pos[:, :TOP_K].reshape(-1)
        xs = _dispatch(pos_flat, x1p, n_rows, n_valid)
        ys = _experts(xs, be[0, :n_blocks], nu[0, :1], w_gate, w_up, w_down, l, n_blocks)
        xf, xb = _combine(pos_flat, ys, wsel, x1, x1b, ws_gate_b, ws_up_b, ws_down_b, ln2_g3, ln2_b3, l, alpha)

        kvw = KV_A * HEAD_DIM
        for lst, (name, w, shp) in zip(st[:5], (("k_a", kvw, (KV_A, HEAD_DIM)), ("v_a", kvw, (KV_A, HEAD_DIM)),
                                                ("ik", IDX_DIM, (IDX_DIM,)), ("k_c", C_W, (H_C, HEAD_DIM)),
                                                ("v_c", C_W, (H_C, HEAD_DIM)))):
            lst.append(cut(name, 0, mp, w).reshape(batch, seq, *shp))
        for lst, (name, w, shp) in zip(st[5:10], (("k_a", kvw, (KV_A, HEAD_DIM)), ("v_a", kvw, (KV_A, HEAD_DIM)),
                                                  ("ik", IDX_DIM, (IDX_DIM,)), ("k_c", C_W, (H_C, HEAD_DIM)),
                                                  ("v_c", C_W, (H_C, HEAD_DIM)))):
            lst.append(cut(name, mp, mp + ms, w).reshape(dec_batch, n_new, *shp))
        st[10].append(unpad(vn_s).reshape(dec_batch, n_new, W_B))

    y_prompt = xf[:mp].reshape(batch, seq, d)
    y_sample = xf[mp:mp + ms].reshape(dec_batch, n_new, d)
    return (y_prompt, y_sample) + tuple(jnp.stack(s) for s in st)
```

```python
import functools

import jax
import jax.numpy as jnp
from jax import lax
from jax.experimental import pallas as pl
from jax.experimental.pallas import tpu as pltpu

F32 = jnp.float32
BF16 = jnp.bfloat16
I32 = jnp.int32

HEAD_DIM = 128
H_A = 6
KV_A = 2
IDX_H = 16
IDX_DIM = 128
TOPK_MAX = 256
CHUNK = 128
G_B = 4
W_B = 512
H_C = 6
QB = 128
TOP_K = 6
ROUTED_SCALE = 2.5
LN_EPS = 1e-5
A_W = H_A * HEAD_DIM
C_W = H_C * HEAD_DIM
LANES = 128
TM = 512
IN_PROJ_MAX_TN = 2304
EXPERT_BLK = 256
ROUTE_TR = 256
DISPATCH_TD = 256
COMBINE_TC = 128
DSA_KT = 256
STICK_HG = 3
STICK_EXIT = -110.0
PAGE_GROUP = 8
NEG = -1e30
INT_MIN = -2147483648
VMEM_LIMIT = 56 * 1024 * 1024


def _cp(sem, vmem=VMEM_LIMIT):
    return pltpu.CompilerParams(dimension_semantics=sem, vmem_limit_bytes=vmem)


def _layout(d_model):
    pieces = (("iq", IDX_H * IDX_DIM), ("g_a", d_model), ("g_b", d_model), ("g_c", d_model),
              ("k_a", KV_A * HEAD_DIM), ("q_a", A_W), ("q_c", C_W), ("k_c", C_W), ("v_c", C_W),
              ("v_a", KV_A * HEAD_DIM), ("u_b", W_B), ("v_b", W_B), ("ik", IDX_DIM), ("iw", LANES))
    src_order = (("q_a", A_W), ("k_a", KV_A * HEAD_DIM), ("v_a", KV_A * HEAD_DIM), ("iq", IDX_H * IDX_DIM),
                 ("ik", IDX_DIM), ("iw", IDX_H), ("u_b", W_B), ("v_b", W_B), ("q_c", C_W), ("k_c", C_W),
                 ("v_c", C_W), ("g_a", d_model), ("g_b", d_model), ("g_c", d_model))
    src, acc = {}, 0
    for name, w in src_order:
        src[name] = (acc, w)
        acc += w
    off, acc = {}, 0
    for name, w in pieces:
        assert acc % w == 0, (name, acc, w)
        off[name] = acc
        acc += w
    return pieces, src, off, acc


def _sortable(x):
    bits = lax.bitcast_convert_type(x, I32)
    return jnp.where(bits < 0, bits ^ 0x7FFFFFFF, bits)


def _gelu(x):
    return 0.5 * x * (1.0 + lax.erf(x * (2.0 ** -0.5)))


def _layer_norm(y, g, b):
    mu = jnp.mean(y, axis=-1, keepdims=True)
    yc = y - mu
    var = jnp.mean(yc * yc, axis=-1, keepdims=True)
    return yc * lax.rsqrt(var + LN_EPS) * g + b


def _dot(a, b):
    return jnp.dot(a, b, preferred_element_type=F32)


def _dot_nt(a, b):
    return lax.dot_general(a, b, (((1,), (1,)), ((), ())), preferred_element_type=F32)


def _pack_halves(xb):
    half = xb.shape[1] // 2
    bits = lax.bitcast_convert_type(xb.astype(F32), I32)
    return lax.shift_right_logical(bits[:, :half], 16) | (bits[:, half:] & jnp.int32(-65536))


def _unpack_halves(u):
    lo = lax.bitcast_convert_type(lax.shift_left(u, 16), F32).astype(BF16)
    hi = lax.bitcast_convert_type(u & jnp.int32(-65536), F32).astype(BF16)
    return lo, hi


def _log_sig_pair(z):
    sp = jnp.log1p(jnp.exp(-jnp.abs(z)))
    return -(jnp.maximum(z, 0.0) + sp), -(jnp.maximum(-z, 0.0) + sp)


def _suffix_matrix():
    r = lax.broadcasted_iota(I32, (LANES, 2 * LANES), 0)
    c = lax.broadcasted_iota(I32, (LANES, 2 * LANES), 1)
    return jnp.where((c >= LANES) | (r > c), 1.0, 0.0).astype(BF16)


def _suffix_sums(lk, u2):
    hi = lk.astype(BF16)
    lo = (lk - hi.astype(F32)).astype(BF16)
    cs = _dot(hi, u2) + _dot(lo, u2)
    return cs[:, :LANES], cs[:, LANES:]


def _matmul_kernel(x_ref, w_ref, o_ref):
    o_ref[...] = _dot(x_ref[...], w_ref[...])


def _in_proj(xb, w_r, layer):
    mt, d = xb.shape
    n = w_r.shape[2]
    tn = max(t for t in range(LANES, IN_PROJ_MAX_TN + 1, LANES) if n % t == 0)
    return pl.pallas_call(
        _matmul_kernel,
        grid=(n // tn, mt // TM),
        in_specs=[pl.BlockSpec((TM, d), lambda j, i: (i, 0)),
                  pl.BlockSpec((None, d, tn), lambda j, i: (layer, 0, j))],
        out_specs=pl.BlockSpec((TM, tn), lambda j, i: (i, j)),
        out_shape=jax.ShapeDtypeStruct((mt, n), F32),
        compiler_params=_cp(("arbitrary", "arbitrary")),
        name="in_proj",
    )(xb, w_r)


def _dsa_prompt_kernel(q_ref, iq_ref, iw_ref, k_ref, v_ref, ik_ref, base_ref, o_ref,
                       iqt_s, key_s, vt_s, qb_s, ot_s, *, seq, topk):
    del base_ref
    i = pl.program_id(1)
    n_t = (i * QB + QB + DSA_KT - 1) // DSA_KT

    def key_rows(j):
        return pl.ds(pl.multiple_of(j * DSA_KT, DSA_KT), DSA_KT)

    @pl.when(i == 0)
    def _():
        def body(j, c):
            for g in range(KV_A):
                vt_s[g, j] = v_ref[key_rows(j), g * HEAD_DIM:(g + 1) * HEAD_DIM].T.astype(BF16)
            return c
        lax.fori_loop(0, seq // DSA_KT, body, 0)

    for h in range(IDX_H):
        iqt_s[:, h * QB:(h + 1) * QB] = iq_ref[:, h * IDX_DIM:(h + 1) * IDX_DIM].T.astype(BF16)
    qb_s[...] = q_ref[...].astype(BF16)
    iwt = iw_ref[...].T * ((IDX_DIM ** -0.5) * (IDX_H ** -0.5))

    row = lax.broadcasted_iota(I32, (DSA_KT, QB), 0)
    col = lax.broadcasted_iota(I32, (DSA_KT, QB), 1)

    def causal(j):
        return (j * DSA_KT + row) <= (i * QB + col)

    def score_body(j, c):
        ikt = ik_ref[key_rows(j), :].astype(BF16)
        s_all = _dot(ikt, iqt_s[...])
        acc = jnp.zeros((DSA_KT, QB), F32)
        for h in range(IDX_H):
            acc = acc + jnp.maximum(s_all[:, h * QB:(h + 1) * QB], 0.0) * iwt[h:h + 1, :]
        key_s[j] = _sortable(jnp.where(causal(j), acc, -jnp.inf))
        return c

    lax.fori_loop(0, n_t, score_body, 0)

    def bit_body(bi, tu):
        candu = tu | lax.shift_left(jnp.int32(1), 31 - bi)
        cand = candu ^ INT_MIN

        def count_body(j, cnt):
            for half in range(DSA_KT // QB):
                cnt = cnt + jnp.where(key_s[j, half * QB:(half + 1) * QB, :] >= cand, 1.0, 0.0)
            return cnt

        cnt = lax.fori_loop(0, n_t, count_body, jnp.zeros((QB, QB), F32))
        return jnp.where(jnp.sum(cnt, axis=0, keepdims=True) >= topk, candu, tu)

    thr = lax.fori_loop(0, 32, bit_body, jnp.zeros((1, QB), I32)) ^ INT_MIN

    ot_s[...] = jnp.zeros(ot_s.shape, F32)
    scale = HEAD_DIM ** -0.5
    rep = H_A // KV_A

    def logits(j):
        sel = (key_s[j] >= thr) & causal(j)
        out = []
        for g in range(KV_A):
            kt = k_ref[key_rows(j), g * HEAD_DIM:(g + 1) * HEAD_DIM].astype(BF16)
            for r in range(rep):
                h = g * rep + r
                out.append(_dot_nt(kt, qb_s[:, h * HEAD_DIM:(h + 1) * HEAD_DIM]) * scale)
        return sel, out

    def max_body(j, ms):
        sel, sts = logits(j)
        return tuple(jnp.maximum(m, jnp.max(jnp.where(sel, st, NEG), axis=0, keepdims=True))
                     for m, st in zip(ms, sts))

    ms = lax.fori_loop(0, n_t, max_body, tuple(jnp.full((1, QB), NEG, F32) for _ in range(H_A)))

    def att_body(j, ls):
        sel, sts = logits(j)
        out = []
        for h in range(H_A):
            p = jnp.where(sel, jnp.exp(sts[h] - ms[h]), 0.0)
            out.append(ls[h] + jnp.sum(p, axis=0, keepdims=True))
            ot_s[h] += _dot(vt_s[h // rep, j], p.astype(BF16))
        return tuple(out)

    ls = lax.fori_loop(0, n_t, att_body, tuple(jnp.zeros((1, QB), F32) for _ in range(H_A)))

    for h in range(H_A):
        o_ref[:, h * HEAD_DIM:(h + 1) * HEAD_DIM] = (ot_s[h] / ls[h]).T.astype(o_ref.dtype)


def _zero_rows(mt, width):
    return jnp.zeros((mt, width), BF16)


def _dsa_prompt(hp, off, batch, seq):
    nq = seq // QB
    topk = min(TOPK_MAX, seq // 4)
    kvw = KV_A * HEAD_DIM
    iqw = IDX_H * IDX_DIM
    return pl.pallas_call(
        functools.partial(_dsa_prompt_kernel, seq=seq, topk=topk),
        grid=(batch, nq),
        in_specs=[pl.BlockSpec((QB, A_W), lambda b, i: (b * nq + i, off["q_a"] // A_W)),
                  pl.BlockSpec((QB, iqw), lambda b, i: (b * nq + i, off["iq"] // iqw)),
                  pl.BlockSpec((QB, LANES), lambda b, i: (b * nq + i, off["iw"] // LANES)),
                  pl.BlockSpec((seq, kvw), lambda b, i: (b, off["k_a"] // kvw)),
                  pl.BlockSpec((seq, kvw), lambda b, i: (b, off["v_a"] // kvw)),
                  pl.BlockSpec((seq, IDX_DIM), lambda b, i: (b, off["ik"] // IDX_DIM)),
                  pl.BlockSpec(memory_space=pl.ANY)],
        out_specs=pl.BlockSpec((QB, A_W), lambda b, i: (b * nq + i, 0)),
        out_shape=jax.ShapeDtypeStruct((hp.shape[0], A_W), BF16),
        input_output_aliases={6: 0},
        scratch_shapes=[pltpu.VMEM((IDX_DIM, IDX_H * QB), BF16),
                        pltpu.VMEM((seq // DSA_KT, DSA_KT, QB), I32),
                        pltpu.VMEM((KV_A, seq // DSA_KT, HEAD_DIM, DSA_KT), BF16),
                        pltpu.VMEM((QB, A_W), BF16),
                        pltpu.VMEM((H_A, HEAD_DIM, QB), F32)],
        compiler_params=_cp(("arbitrary", "arbitrary")),
        name="dsa_prompt",
    )(hp, hp, hp, hp, hp, hp, _zero_rows(hp.shape[0], A_W))


def _stick_prompt_kernel(q_ref, k_ref, v_ref, base_ref, o_ref):
    del base_ref
    i = pl.program_id(2)
    u2 = _suffix_matrix()
    row = lax.broadcasted_iota(I32, (QB, QB), 0)
    col = lax.broadcasted_iota(I32, (QB, QB), 1)
    scale = HEAD_DIM ** -0.5
    heads = [slice(h * HEAD_DIM, (h + 1) * HEAD_DIM) for h in range(STICK_HG)]
    qs = [q_ref[:, hs].astype(BF16) for hs in heads]

    def cond(c):
        return (c[0] <= i) & (c[1] == 0)

    def body(c):
        jj, _, accs, tots = c
        j = i - jj
        rows = pl.ds(pl.multiple_of(j * QB, QB), QB)
        mask = (j * QB + col) < (i * QB + row)
        new_accs, new_tots = [], []
        for q, hs, acc, tot in zip(qs, heads, accs, tots):
            z = _dot_nt(q, k_ref[rows, hs].astype(BF16)) * scale
            lk, ls = _log_sig_pair(z)
            lk = jnp.where(mask, lk, 0.0)
            within, total = _suffix_sums(lk, u2)
            a = jnp.where(mask, jnp.exp(ls + within + tot), 0.0)
            new_accs.append(acc + _dot(a.astype(BF16), v_ref[rows, hs].astype(BF16)))
            new_tots.append(tot + total)
        worst = functools.reduce(jnp.maximum, new_tots)
        done = (jnp.max(worst) < STICK_EXIT).astype(I32)
        return jj + 1, done, tuple(new_accs), tuple(new_tots)

    zeros = tuple(jnp.zeros((QB, QB), F32) for _ in range(STICK_HG))
    _, _, accs, _ = lax.while_loop(cond, body, (jnp.int32(0), jnp.int32(0), zeros, zeros))
    for hs, acc in zip(heads, accs):
        o_ref[:, hs] = acc.astype(o_ref.dtype)


def _stick_prompt(hp, off, batch, seq):
    nq = seq // QB
    w = STICK_HG * HEAD_DIM
    assert all(off[n] % w == 0 for n in ("q_c", "k_c", "v_c"))
    return pl.pallas_call(
        _stick_prompt_kernel,
        grid=(batch, H_C // STICK_HG, nq),
        in_specs=[pl.BlockSpec((QB, w), lambda b, h, i: (b * nq + i, off["q_c"] // w + h)),
                  pl.BlockSpec((seq, w), lambda b, h, i: (b, off["k_c"] // w + h)),
                  pl.BlockSpec((seq, w), lambda b, h, i: (b, off["v_c"] // w + h)),
                  pl.BlockSpec(memory_space=pl.ANY)],
        out_specs=pl.BlockSpec((QB, w), lambda b, h, i: (b * nq + i, h)),
        out_shape=jax.ShapeDtypeStruct((hp.shape[0], C_W), BF16),
        input_output_aliases={3: 0},
        compiler_params=_cp(("arbitrary", "arbitrary", "arbitrary")),
        name="stick_prompt",
    )(hp, hp, hp, _zero_rows(hp.shape[0], C_W))


def _gmlp_kernel(u_ref, v_ref, ws_ref, bst_ref, g_ref, b_ref, *rest):
    o_ref, vn_ref = rest[-2:]
    vn = _layer_norm(_gelu(v_ref[...]), g_ref[...], b_ref[...])
    vn_ref[...] = vn
    gu = _gelu(u_ref[...])
    row = lax.broadcasted_iota(I32, (CHUNK, CHUNK), 0)
    col = lax.broadcasted_iota(I32, (CHUNK, CHUNK), 1)
    gw = W_B // G_B
    for g in range(G_B):
        w = jnp.where(row >= col, ws_ref[g], 0.0).astype(BF16)
        mixed = _dot(w, vn[:, g * gw:(g + 1) * gw].astype(BF16)) + bst_ref[:, g:g + 1]
        o_ref[:, g * gw:(g + 1) * gw] = (gu[:, g * gw:(g + 1) * gw] * mixed).astype(o_ref.dtype)


def _gmlp(u_src, v_src, u_blk, v_blk, n_chunks, w_s, b_s_t, g, b, out_rows=None):
    base = [] if out_rows is None else [_zero_rows(out_rows, W_B)]
    return pl.pallas_call(
        _gmlp_kernel,
        grid=(n_chunks,),
        in_specs=[pl.BlockSpec((CHUNK, W_B), lambda c: (c, u_blk)),
                  pl.BlockSpec((CHUNK, W_B), lambda c: (c, v_blk)),
                  pl.BlockSpec((G_B, CHUNK, CHUNK), lambda c: (0, 0, 0)),
                  pl.BlockSpec((CHUNK, G_B), lambda c: (0, 0)),
                  pl.BlockSpec((1, W_B), lambda c: (0, 0)),
                  pl.BlockSpec((1, W_B), lambda c: (0, 0))]
        + [pl.BlockSpec(memory_space=pl.ANY) for _ in base],
        out_specs=[pl.BlockSpec((CHUNK, W_B), lambda c: (c, 0)),
                   pl.BlockSpec((CHUNK, W_B), lambda c: (c, 0))],
        out_shape=[jax.ShapeDtypeStruct((out_rows or n_chunks * CHUNK, W_B), BF16),
                   jax.ShapeDtypeStruct((n_chunks * CHUNK, W_B), F32)],
        input_output_aliases={6: 0} if base else {},
        compiler_params=_cp(("arbitrary",)),
        name="gmlp",
    )(u_src, v_src, w_s, b_s_t, g, b, *base)


def _page_specs(block, n_pages, layer, reverse=False):
    def spec(g):
        def index(b, p, pt):
            k = p * PAGE_GROUP + g
            k = n_pages - 1 - k if reverse else k
            return (layer, pt[b * n_pages + k]) + (0,) * (len(block) - 2)
        return pl.BlockSpec(block, index)
    return [spec(g) for g in range(PAGE_GROUP)]


def _dsa_sample_score_kernel(pt_ref, iq_ref, wrow_ref, iknew_ref, *rest, n_groups, n_new, topk):
    del pt_ref
    ik_refs = rest[:PAGE_GROUP]
    i_ref, thr_ref, q2_s, key_s = rest[PAGE_GROUP:]
    p = pl.program_id(1)
    gw = PAGE_GROUP * LANES

    @pl.when(p == 0)
    def _():
        q2 = jnp.concatenate([iq_ref[:, h * IDX_DIM:(h + 1) * IDX_DIM] for h in range(IDX_H)], axis=0)
        q2_s[...] = q2.astype(BF16)

    wrow = wrow_ref[0:1, :] * ((IDX_DIM ** -0.5) * (IDX_H ** -0.5))
    r16 = lax.broadcasted_iota(I32, (2 * n_new, LANES), 0)
    l16 = lax.broadcasted_iota(I32, (2 * n_new, LANES), 1)
    head_sum = jnp.where(l16 % n_new == r16, 1.0, 0.0).astype(BF16)

    def scores(keys):
        r = jnp.maximum(_dot_nt(keys.astype(BF16), q2_s[...]), 0.0) * wrow
        hi = r.astype(BF16)
        lo = (r - hi.astype(F32)).astype(BF16)
        return (_dot_nt(head_sum, hi) + _dot_nt(head_sum, lo))[0:n_new, :]

    it = scores(jnp.concatenate([ref[...] for ref in ik_refs], axis=0))
    i_ref[p] = it
    key_s[p] = _sortable(it)

    @pl.when(p == n_groups - 1)
    def _():
        pad = jnp.zeros((LANES - n_new, IDX_DIM), F32)
        itn = scores(jnp.concatenate([iknew_ref[...], pad], axis=0))
        t = lax.broadcasted_iota(I32, (n_new, LANES), 0)
        s = lax.broadcasted_iota(I32, (n_new, LANES), 1)
        itn = jnp.where((s <= t) & (s < n_new), itn, -jnp.inf)
        itn = jnp.concatenate([itn, jnp.full((n_new, gw - LANES), -jnp.inf, F32)], axis=1)
        i_ref[n_groups] = itn
        key_s[n_groups] = _sortable(itn)

        def bit_body(bi, tu):
            candu = tu | lax.shift_left(jnp.int32(1), 31 - bi)
            cand = candu ^ INT_MIN

            def count_body(j, cnt):
                return cnt + jnp.where(key_s[j] >= cand, 1.0, 0.0)

            cnt = lax.fori_loop(0, n_groups + 1, count_body, jnp.zeros((n_new, gw), F32))
            return jnp.where(jnp.sum(cnt, axis=1, keepdims=True) >= topk, candu, tu)

        thr = lax.fori_loop(0, 32, bit_body, jnp.zeros((n_new, 1), I32)) ^ INT_MIN
        thr_ref[...] = jnp.broadcast_to(thr, thr_ref.shape)


def _dsa_sample_scores(hp, wrow, cache_idx_k, pt_flat, off, layer, mp, dec_batch, n_new, n_pages):
    topk = min(TOPK_MAX, (n_pages * LANES + n_new) // 4)
    iqw = IDX_H * IDX_DIM
    rb = mp // n_new
    n_groups = n_pages // PAGE_GROUP
    gw = PAGE_GROUP * LANES
    grid_spec = pltpu.PrefetchScalarGridSpec(
        num_scalar_prefetch=1,
        grid=(dec_batch, n_groups),
        in_specs=[pl.BlockSpec((n_new, iqw), lambda b, p, pt: (rb + b, off["iq"] // iqw)),
                  pl.BlockSpec((n_new, LANES), lambda b, p, pt: (b, 0)),
                  pl.BlockSpec((n_new, IDX_DIM), lambda b, p, pt: (rb + b, off["ik"] // IDX_DIM))]
        + _page_specs((None, None, LANES, IDX_DIM), n_pages, layer),
        out_specs=[pl.BlockSpec((None, n_groups + 1, n_new, gw), lambda b, p, pt: (b, 0, 0, 0)),
                   pl.BlockSpec((n_new, LANES), lambda b, p, pt: (b, 0))],
        scratch_shapes=[pltpu.VMEM((IDX_H * n_new, IDX_DIM), BF16),
                        pltpu.VMEM((n_groups + 1, n_new, gw), I32)])
    return pl.pallas_call(
        functools.partial(_dsa_sample_score_kernel, n_groups=n_groups, n_new=n_new, topk=topk),
        grid_spec=grid_spec,
        out_shape=[jax.ShapeDtypeStruct((dec_batch, n_groups + 1, n_new, gw), F32),
                   jax.ShapeDtypeStruct((dec_batch * n_new, LANES), I32)],
        compiler_params=_cp(("arbitrary", "arbitrary")),
        name="dsa_sample_scores",
    )(pt_flat, hp, wrow, hp, *([cache_idx_k] * PAGE_GROUP))


def _dsa_sample_attend_kernel(pt_ref, q_ref, i_ref, inew_ref, thr_ref, knew_ref, vnew_ref, *rest, n_groups, n_new):
    del pt_ref
    k_refs, v_refs = rest[:PAGE_GROUP], rest[PAGE_GROUP:2 * PAGE_GROUP]
    o_ref, qbd_s, o_s, m_s, l_s = rest[2 * PAGE_GROUP:]
    p = pl.program_id(1)
    rep = H_A // KV_A
    n_rows = H_A * n_new
    grp = lax.broadcasted_iota(I32, (n_rows, 1), 0) // (rep * n_new)
    scale = HEAD_DIM ** -0.5

    @pl.when(p == 0)
    def _():
        qs = jnp.concatenate([q_ref[:, h * HEAD_DIM:(h + 1) * HEAD_DIM] for h in range(H_A)], axis=0)
        qbd_s[...] = jnp.concatenate([jnp.where(grp == g, qs, 0.0) for g in range(KV_A)], axis=1).astype(BF16)
        m_s[...] = jnp.full(m_s.shape, NEG, F32)
        l_s[...] = jnp.zeros(l_s.shape, F32)
        o_s[...] = jnp.zeros(o_s.shape, F32)

    def update(sel_q, kflat, vstack):
        sel = jnp.concatenate([jnp.where(sel_q, 1.0, 0.0)] * H_A, axis=0) > 0.5
        s = _dot_nt(qbd_s[...], kflat.astype(BF16)) * scale
        m_old = m_s[...]
        m_new = jnp.maximum(m_old, jnp.max(jnp.where(sel, s, NEG), axis=1, keepdims=True))
        pr = jnp.where(sel, jnp.exp(s - m_new), 0.0)
        alpha = jnp.exp(m_old - m_new)
        l_s[...] = alpha * l_s[...] + jnp.sum(pr, axis=1, keepdims=True)
        pbd = jnp.concatenate([jnp.where(grp == g, pr, 0.0) for g in range(KV_A)], axis=1).astype(BF16)
        o_s[...] = alpha * o_s[...] + _dot(pbd, vstack.astype(BF16))
        m_s[...] = m_new

    thr = thr_ref[:, 0:1]
    kflat = jnp.concatenate([jnp.concatenate([k[:, g, :] for g in range(KV_A)], axis=1) for k in k_refs], axis=0)
    vstack = jnp.concatenate([v[:, g, :] for g in range(KV_A) for v in v_refs], axis=0)
    update(_sortable(i_ref[...]) >= thr, kflat, vstack)

    @pl.when(p == n_groups - 1)
    def _():
        t = lax.broadcasted_iota(I32, (n_new, LANES), 0)
        s = lax.broadcasted_iota(I32, (n_new, LANES), 1)
        sel_new = (_sortable(inew_ref[:, 0:LANES]) >= thr) & (s <= t) & (s < n_new)
        pad = jnp.zeros((LANES - n_new, KV_A * HEAD_DIM), F32)
        kn = jnp.concatenate([knew_ref[...], pad], axis=0)
        vn = jnp.concatenate([vnew_ref[...], pad], axis=0)
        vns = jnp.concatenate([vn[:, g * HEAD_DIM:(g + 1) * HEAD_DIM] for g in range(KV_A)], axis=0)
        update(sel_new, kn, vns)
        o = o_s[...] / l_s[...]
        for h in range(H_A):
            o_ref[:, h * HEAD_DIM:(h + 1) * HEAD_DIM] = o[h * n_new:(h + 1) * n_new, :]


def _dsa_sample_attend(hp, scores, thr, cache_k_a, cache_v_a, pt_flat, off, layer, mp, dec_batch, n_new, n_pages):
    kvw = KV_A * HEAD_DIM
    rb = mp // n_new
    n_rows = H_A * n_new
    n_groups = n_pages // PAGE_GROUP
    gw = PAGE_GROUP * LANES
    page_block = (None, None, LANES, KV_A, HEAD_DIM)
    grid_spec = pltpu.PrefetchScalarGridSpec(
        num_scalar_prefetch=1,
        grid=(dec_batch, n_groups),
        in_specs=[pl.BlockSpec((n_new, A_W), lambda b, p, pt: (rb + b, off["q_a"] // A_W)),
                  pl.BlockSpec((None, None, n_new, gw), lambda b, p, pt: (b, p, 0, 0)),
                  pl.BlockSpec((None, None, n_new, gw), lambda b, p, pt: (b, n_groups, 0, 0)),
                  pl.BlockSpec((n_new, LANES), lambda b, p, pt: (b, 0)),
                  pl.BlockSpec((n_new, kvw), lambda b, p, pt: (rb + b, off["k_a"] // kvw)),
                  pl.BlockSpec((n_new, kvw), lambda b, p, pt: (rb + b, off["v_a"] // kvw))]
        + _page_specs(page_block, n_pages, layer) + _page_specs(page_block, n_pages, layer),
        out_specs=pl.BlockSpec((n_new, A_W), lambda b, p, pt: (b, 0)),
        scratch_shapes=[pltpu.VMEM((n_rows, KV_A * HEAD_DIM), BF16),
                        pltpu.VMEM((n_rows, HEAD_DIM), F32),
                        pltpu.VMEM((n_rows, 1), F32),
                        pltpu.VMEM((n_rows, 1), F32)])
    return pl.pallas_call(
        functools.partial(_dsa_sample_attend_kernel, n_groups=n_groups, n_new=n_new),
        grid_spec=grid_spec,
        out_shape=jax.ShapeDtypeStruct((dec_batch * n_new, A_W), F32),
        compiler_params=_cp(("arbitrary", "arbitrary")),
        name="dsa_sample_attend",
    )(pt_flat, hp, scores, scores, thr, hp, hp, *([cache_k_a] * PAGE_GROUP), *([cache_v_a] * PAGE_GROUP))


def _stick_sample_kernel(pt_ref, q_ref, knew_ref, vnew_ref, kc_ref, vc_ref, o_ref,
                         kbuf, vbuf, sem, qbd_s, acc_s, tot_s, *, layer, n_pages, n_new):
    b = pl.program_id(0)
    n_groups = n_pages // PAGE_GROUP
    n_rows = H_C * n_new
    head = lax.broadcasted_iota(I32, (n_rows, LANES), 0) // n_new
    u2 = _suffix_matrix()
    scale = HEAD_DIM ** -0.5

    def sweep(mask, kflat, vstack):
        z = _dot_nt(qbd_s[...], kflat.astype(BF16)) * scale
        lk, ls = _log_sig_pair(z)
        if mask is not None:
            lk = jnp.where(mask, lk, 0.0)
        within, total = _suffix_sums(lk, u2)
        a = jnp.exp(ls + within + tot_s[...])
        if mask is not None:
            a = jnp.where(mask, a, 0.0)
        abd = jnp.concatenate([jnp.where(head == h, a, 0.0) for h in range(H_C)], axis=1).astype(BF16)
        acc_s[...] = acc_s[...] + _dot(abd, vstack.astype(BF16))
        tot_s[...] = tot_s[...] + total

    qs = jnp.concatenate([q_ref[:, h * HEAD_DIM:(h + 1) * HEAD_DIM] for h in range(H_C)], axis=0)
    qbd_s[...] = jnp.concatenate([jnp.where(head == h, qs, 0.0) for h in range(H_C)], axis=1).astype(BF16)
    acc_s[...] = jnp.zeros(acc_s.shape, F32)
    tot_s[...] = jnp.zeros(tot_s.shape, F32)
    t = lax.broadcasted_iota(I32, (n_rows, LANES), 0) % n_new
    s = lax.broadcasted_iota(I32, (n_rows, LANES), 1)
    pad = jnp.zeros((LANES - n_new, C_W), F32)
    kn = jnp.concatenate([knew_ref[...], pad], axis=0)
    vn = jnp.concatenate([vnew_ref[...], pad], axis=0)
    vns = jnp.concatenate([vn[:, h * HEAD_DIM:(h + 1) * HEAD_DIM] for h in range(H_C)], axis=0)
    sweep(s < t, kn, vns)

    def page_copies(grp):
        out = []
        for g in range(PAGE_GROUP):
            page = pt_ref[b * n_pages + n_pages - 1 - (grp * PAGE_GROUP + g)]
            out.append(pltpu.make_async_copy(kc_ref.at[layer, page], kbuf.at[g], sem.at[0]))
            out.append(pltpu.make_async_copy(vc_ref.at[layer, page], vbuf.at[g], sem.at[1]))
        return out

    def all_zero_from_here():
        return (jnp.max(tot_s[...]) < STICK_EXIT).astype(I32)

    def cond(c):
        return (c[0] < n_groups) & (c[1] == 0)

    def body(c):
        grp = c[0]
        for cp in page_copies(grp):
            cp.start()
        for cp in page_copies(grp):
            cp.wait()
        for g in range(PAGE_GROUP):
            kflat = jnp.concatenate([kbuf[g, h] for h in range(H_C)], axis=1)
            vstack = jnp.concatenate([vbuf[g, h] for h in range(H_C)], axis=0)
            sweep(None, kflat, vstack)
        return grp + 1, all_zero_from_here()

    lax.while_loop(cond, body, (jnp.int32(0), all_zero_from_here()))

    acc = acc_s[...]
    for h in range(H_C):
        o_ref[:, h * HEAD_DIM:(h + 1) * HEAD_DIM] = acc[h * n_new:(h + 1) * n_new, :]


def _stick_sample(hp, cache_k_c_t, cache_v_c_t, pt_flat, off, layer, mp, dec_batch, n_new, n_pages):
    rb = mp // n_new
    n_rows = H_C * n_new
    page_buf = pltpu.VMEM((PAGE_GROUP, H_C, LANES, HEAD_DIM), F32)
    grid_spec = pltpu.PrefetchScalarGridSpec(
        num_scalar_prefetch=1,
        grid=(dec_batch,),
        in_specs=[pl.BlockSpec((n_new, C_W), lambda b, pt: (rb + b, off["q_c"] // C_W)),
                  pl.BlockSpec((n_new, C_W), lambda b, pt: (rb + b, off["k_c"] // C_W)),
                  pl.BlockSpec((n_new, C_W), lambda b, pt: (rb + b, off["v_c"] // C_W)),
                  pl.BlockSpec(memory_space=pl.ANY),
                  pl.BlockSpec(memory_space=pl.ANY)],
        out_specs=pl.BlockSpec((n_new, C_W), lambda b, pt: (b, 0)),
        scratch_shapes=[page_buf, page_buf, pltpu.SemaphoreType.DMA((2,)),
                        pltpu.VMEM((n_rows, C_W), BF16),
                        pltpu.VMEM((n_rows, HEAD_DIM), F32),
                        pltpu.VMEM((n_rows, LANES), F32)])
    return pl.pallas_call(
        functools.partial(_stick_sample_kernel, layer=layer, n_pages=n_pages, n_new=n_new),
        grid_spec=grid_spec,
        out_shape=jax.ShapeDtypeStruct((dec_batch * n_new, C_W), F32),
        compiler_params=_cp(("arbitrary",)),
        name="stick_sample",
    )(pt_flat, hp, hp, hp, cache_k_c_t, cache_v_c_t)


def _merge_kernel(oa_ref, ob_ref, oc_ref, ga_ref, gb_ref, gc_ref, wpa_ref, wpb_ref, wpc_ref, m_ref):
    m = (jax.nn.sigmoid(ga_ref[...]) * _dot(oa_ref[...], wpa_ref[...])
         + jax.nn.sigmoid(gb_ref[...]) * _dot(ob_ref[...], wpb_ref[...])
         + jax.nn.sigmoid(gc_ref[...]) * _dot(oc_ref[...], wpc_ref[...]))
    m_ref[...] = m.astype(m_ref.dtype)


def _merge(o_a, o_b, o_c, hp, off, w_pa, w_pb, w_pc, layer):
    mt = hp.shape[0]
    d = w_pa.shape[2]
    gate = lambda name: pl.BlockSpec((TM, d), lambda i: (i, off[name] // d))
    wspec = lambda k: pl.BlockSpec((None, k, d), lambda i: (layer, 0, 0))
    return pl.pallas_call(
        _merge_kernel,
        grid=(mt // TM,),
        in_specs=[pl.BlockSpec((TM, A_W), lambda i: (i, 0)),
                  pl.BlockSpec((TM, W_B), lambda i: (i, 0)),
                  pl.BlockSpec((TM, C_W), lambda i: (i, 0)),
                  gate("g_a"), gate("g_b"), gate("g_c"),
                  wspec(A_W), wspec(W_B), wspec(C_W)],
        out_specs=pl.BlockSpec((TM, d), lambda i: (i, 0)),
        out_shape=jax.ShapeDtypeStruct((mt, d), BF16),
        compiler_params=_cp(("arbitrary",)),
        name="merge",
    )(o_a, o_b, o_c, hp, hp, hp, w_pa, w_pb, w_pc)


def _post_mix_kernel(m_ref, x_ref, wout_ref, g_ref, b_ref, wr_ref, br_ref,
                     x1_ref, x1b_ref, x1p_ref, idx_ref, wsel_ref, *, alpha, n_valid, n_experts):
    x1 = _layer_norm(alpha * x_ref[...] + _dot(m_ref[...], wout_ref[...]), g_ref[...], b_ref[...])
    x1_ref[...] = x1
    x1b = x1.astype(BF16)
    x1b_ref[...] = x1b
    x1p_ref[...] = _pack_halves(x1b)
    scores = jax.nn.sigmoid(_dot(x1b, wr_ref[...]))
    lane = lax.broadcasted_iota(I32, scores.shape, 1)
    lane_f = lane.astype(F32)
    sel = jnp.where(lane < n_experts, scores + br_ref[...], -jnp.inf)
    idx = jnp.full(scores.shape, -1.0, F32)
    wts = jnp.zeros(scores.shape, F32)
    for k in range(TOP_K):
        mx = jnp.max(sel, axis=1, keepdims=True)
        ik = jnp.min(jnp.where(sel == mx, lane_f, float(LANES)), axis=1, keepdims=True)
        hit = lane_f == ik
        wk = jnp.sum(jnp.where(hit, scores, 0.0), axis=1, keepdims=True)
        sel = jnp.where(hit, -jnp.inf, sel)
        idx = jnp.where(lane == k, ik, idx)
        wts = jnp.where(lane == k, wk, wts)
    wts = wts / jnp.sum(wts, axis=1, keepdims=True) * ROUTED_SCALE
    rowid = pl.program_id(0) * TM + lax.broadcasted_iota(I32, scores.shape, 0)
    valid = rowid < n_valid
    idx_ref[...] = jnp.where(valid, idx, -1.0).astype(I32)
    wsel_ref[...] = jnp.where(valid, wts, 0.0)


def _post_mix(m, xf, w_out, g, b, w_r, b_r, layer, alpha, n_valid, n_experts):
    mt, d = xf.shape
    row = lambda i: (i, 0)
    vec = lambda n: pl.BlockSpec((None, 1, n), lambda i: (layer, 0, 0))
    return pl.pallas_call(
        functools.partial(_post_mix_kernel, alpha=alpha, n_valid=n_valid, n_experts=n_experts),
        grid=(mt // TM,),
        in_specs=[pl.BlockSpec((TM, d), row), pl.BlockSpec((TM, d), row),
                  pl.BlockSpec((None, d, d), lambda i: (layer, 0, 0)),
                  vec(d), vec(d),
                  pl.BlockSpec((None, d, LANES), lambda i: (layer, 0, 0)),
                  vec(LANES)],
        out_specs=[pl.BlockSpec((TM, d), row), pl.BlockSpec((TM, d), row), pl.BlockSpec((TM, d // 2), row),
                   pl.BlockSpec((TM, LANES), row), pl.BlockSpec((TM, LANES), row)],
        out_shape=[jax.ShapeDtypeStruct((mt, d), F32), jax.ShapeDtypeStruct((mt, d), BF16),
                   jax.ShapeDtypeStruct((mt, d // 2), I32),
                   jax.ShapeDtypeStruct((mt, LANES), I32), jax.ShapeDtypeStruct((mt, LANES), F32)],
        compiler_params=_cp(("arbitrary",)),
        name="post_mix",
    )(m, xf, w_out, g, b, w_r, b_r)


def _route_kernel(idx_ref, pos_ref, be_ref, nu_ref, *, mt, n_experts, nbp):
    n_tiles = mt // ROUTE_TR
    lane = lax.broadcasted_iota(I32, (ROUTE_TR, LANES), 1)

    def one_hot(t):
        idx = idx_ref[pl.ds(pl.multiple_of(t * ROUTE_TR, ROUTE_TR), ROUTE_TR), :]
        oh = jnp.zeros((ROUTE_TR, LANES), F32)
        for k in range(TOP_K):
            oh = oh + jnp.where(idx[:, k:k + 1] == lane, 1.0, 0.0)
        return idx, oh

    def count_body(t, cnt):
        return cnt + jnp.sum(one_hot(t)[1], axis=0, keepdims=True)

    cnt = lax.fori_loop(0, n_tiles, count_body, jnp.zeros((1, LANES), F32))
    nb = jnp.floor((cnt + (EXPERT_BLK - 1)) * (1.0 / EXPERT_BLK))
    r = lax.broadcasted_iota(I32, (LANES, LANES), 0)
    c = lax.broadcasted_iota(I32, (LANES, LANES), 1)
    nb8 = jnp.broadcast_to(nb, (8, LANES))
    start = _dot(nb8.astype(BF16), jnp.where(r < c, 1.0, 0.0).astype(BF16))
    pend = start + nb8
    start_row = start[0:1, :] * float(EXPERT_BLK)

    rr = lax.broadcasted_iota(I32, (ROUTE_TR, ROUTE_TR), 0)
    cc = lax.broadcasted_iota(I32, (ROUTE_TR, ROUTE_TR), 1)
    lower = jnp.where(cc < rr, 1.0, 0.0).astype(BF16)

    def pos_body(t, seen):
        idx, oh = one_hot(t)
        pe = start_row + seen + _dot(lower, oh.astype(BF16))
        pos = jnp.zeros((ROUTE_TR, LANES), F32)
        for k in range(TOP_K):
            pk = jnp.sum(jnp.where(idx[:, k:k + 1] == lane, pe, 0.0), axis=1, keepdims=True)
            pos = jnp.where(lane == k, pk, pos)
        pos_ref[pl.ds(pl.multiple_of(t * ROUTE_TR, ROUTE_TR), ROUTE_TR), :] = pos.astype(I32)
        return seen + jnp.sum(oh, axis=0, keepdims=True)

    lax.fori_loop(0, n_tiles, pos_body, jnp.zeros((1, LANES), F32))

    pend_t = jnp.broadcast_to(pend[0:1, :], (LANES, LANES)).T
    for ch in range(nbp // LANES):
        blk = (ch * LANES + c).astype(F32)
        be = jnp.sum(jnp.where((pend_t <= blk) & (r < n_experts), 1.0, 0.0), axis=0, keepdims=True)
        be = jnp.minimum(be, float(n_experts - 1))
        be_ref[:, ch * LANES:(ch + 1) * LANES] = jnp.broadcast_to(be, (8, LANES)).astype(I32)
    nu_ref[...] = jnp.broadcast_to(jnp.max(pend, axis=1, keepdims=True), (8, LANES)).astype(I32)


def _route(idx, n_experts, n_blocks):
    mt = idx.shape[0]
    nbp = -(-n_blocks // LANES) * LANES
    return pl.pallas_call(
        functools.partial(_route_kernel, mt=mt, n_experts=n_experts, nbp=nbp),
        out_shape=[jax.ShapeDtypeStruct((mt, LANES), I32),
                   jax.ShapeDtypeStruct((8, nbp), I32),
                   jax.ShapeDtypeStruct((8, LANES), I32)],
        compiler_params=pltpu.CompilerParams(vmem_limit_bytes=VMEM_LIMIT),
        name="route",
    )(idx)


def _dispatch_kernel(pos_ref, x_ref, xs_in_ref, xs_ref, sem, *, n_valid):
    del xs_in_ref
    t0 = pl.program_id(0) * DISPATCH_TD
    n = jnp.clip(n_valid - t0, 0, DISPATCH_TD)

    def copy(r, k):
        dst = pos_ref[(t0 + r) * TOP_K + k]
        return pltpu.make_async_copy(x_ref.at[pl.ds(r, 1), :], xs_ref.at[pl.ds(dst, 1), :], sem)

    def start_body(r, c):
        for k in range(TOP_K):
            copy(r, k).start()
        return c

    def wait_body(r, c):
        for k in range(TOP_K):
            copy(r, k).wait()
        return c

    @pl.when(n == DISPATCH_TD)
    def _():
        lax.fori_loop(0, DISPATCH_TD, start_body, 0, unroll=True)
        for _ in range(TOP_K):
            pltpu.make_async_copy(x_ref, xs_ref.at[pl.ds(0, DISPATCH_TD), :], sem).wait()

    @pl.when(n != DISPATCH_TD)
    def _():
        lax.fori_loop(0, n, start_body, 0)
        lax.fori_loop(0, n, wait_body, 0)


def _dispatch(pos_flat, x1p, n_rows, n_valid):
    mt, d = x1p.shape
    grid_spec = pltpu.PrefetchScalarGridSpec(
        num_scalar_prefetch=1,
        grid=(mt // DISPATCH_TD,),
        in_specs=[pl.BlockSpec((DISPATCH_TD, d), lambda i, pos: (i, 0)),
                  pl.BlockSpec(memory_space=pl.ANY)],
        out_specs=pl.BlockSpec(memory_space=pl.ANY),
        scratch_shapes=[pltpu.SemaphoreType.DMA(())])
    return pl.pallas_call(
        functools.partial(_dispatch_kernel, n_valid=n_valid),
        grid_spec=grid_spec,
        out_shape=jax.ShapeDtypeStruct((n_rows, d), I32),
        input_output_aliases={2: 0},
        compiler_params=_cp(("arbitrary",)),
        name="dispatch",
    )(pos_flat, x1p, jnp.zeros((n_rows, d), I32))


def _expert_kernel(be_ref, nu_ref, x_ref, wg_hbm, wu_hbm, wd_hbm, y_ref,
                   wg_f, wu_f, wd_f, wg_s, wu_s, wd_s, sem, slot_s, *, layer, n_blocks):
    i = pl.program_id(0)
    n_used = nu_ref[0]
    used = i < n_used
    expert = be_ref[i]
    new_expert = (i == 0) | (expert != be_ref[jnp.maximum(i - 1, 0)])

    def weight_copies(e, s):
        return [pltpu.make_async_copy(wg_hbm.at[layer, e], wg_f.at[s], sem.at[s, 0]),
                pltpu.make_async_copy(wu_hbm.at[layer, e], wu_f.at[s], sem.at[s, 1]),
                pltpu.make_async_copy(wd_hbm.at[layer, e], wd_f.at[s], sem.at[s, 2])]

    @pl.when(used & (i == 0))
    def _():
        slot_s[0] = 0
        for cp in weight_copies(expert, 0):
            cp.start()

    @pl.when(used & new_expert)
    def _():
        s = slot_s[0]
        for cp in weight_copies(expert, s):
            cp.wait()
        wg_s[...] = wg_f[s].astype(BF16)
        wu_s[...] = wu_f[s].astype(BF16)
        wd_s[...] = wd_f[s].astype(BF16)
        nxt = lax.while_loop(lambda j: (j < n_used) & (be_ref[jnp.minimum(j, n_blocks - 1)] == expert),
                             lambda j: j + 1, i + 1)

        @pl.when(nxt < n_used)
        def _():
            for cp in weight_copies(be_ref[jnp.minimum(nxt, n_blocks - 1)], 1 - s):
                cp.start()
            slot_s[0] = 1 - s

    @pl.when(used)
    def _():
        lo, hi = _unpack_halves(x_ref[...])
        half = lo.shape[1]
        gate = _dot(lo, wg_s[:half, :]) + _dot(hi, wg_s[half:, :])
        up = _dot(lo, wu_s[:half, :]) + _dot(hi, wu_s[half:, :])
        y = _dot((jax.nn.silu(gate) * up).astype(BF16), wd_s[...])
        y_ref[...] = _pack_halves(y.astype(BF16))

    @pl.when(jnp.logical_not(used))
    def _():
        y_ref[...] = jnp.zeros(y_ref.shape, I32)


def _experts(xs, be, nu, w_gate, w_up, w_down, layer, n_blocks):
    n_rows = xs.shape[0]
    d, f = w_gate.shape[2], w_gate.shape[3]
    blk = lambda i, be, nu: (jnp.minimum(i, nu[0] - 1), 0)
    grid_spec = pltpu.PrefetchScalarGridSpec(
        num_scalar_prefetch=2,
        grid=(n_blocks,),
        in_specs=[pl.BlockSpec((EXPERT_BLK, d // 2), blk),
                  pl.BlockSpec(memory_space=pl.ANY),
                  pl.BlockSpec(memory_space=pl.ANY),
                  pl.BlockSpec(memory_space=pl.ANY)],
        out_specs=pl.BlockSpec((EXPERT_BLK, d // 2), lambda i, be, nu: (i, 0)),
        scratch_shapes=[pltpu.VMEM((2, d, f), F32), pltpu.VMEM((2, d, f), F32), pltpu.VMEM((2, f, d), F32),
                        pltpu.VMEM((d, f), BF16), pltpu.VMEM((d, f), BF16), pltpu.VMEM((f, d), BF16),
                        pltpu.SemaphoreType.DMA((2, 3)), pltpu.SMEM((1,), I32)])
    return pl.pallas_call(
        functools.partial(_expert_kernel, layer=layer, n_blocks=n_blocks),
        grid_spec=grid_spec,
        out_shape=jax.ShapeDtypeStruct((n_rows, d // 2), I32),
        compiler_params=_cp(("arbitrary",)),
        name="experts",
    )(be, nu, xs, w_gate, w_up, w_down)


def _combine_kernel(pos_ref, ys_ref, wsel_ref, x1_ref, x1b_ref, wsg_ref, wsu_ref, wsd_ref, g_ref, b_ref,
                    x2_ref, x2b_ref, buf, sem, *, alpha, n_steps):
    i = pl.program_id(0)
    slot = i % 2

    def gather(step, s):
        t0 = step * COMBINE_TC

        def body(r, c):
            for k in range(TOP_K):
                src = pos_ref[(t0 + r) * TOP_K + k]
                pltpu.make_async_copy(ys_ref.at[pl.ds(src, 1), :],
                                      buf.at[s, pl.ds(k * COMBINE_TC + r, 1), :], sem.at[s]).start()
            return c

        lax.fori_loop(0, COMBINE_TC, body, 0, unroll=True)

    def wait_slot(s):
        pltpu.make_async_copy(ys_ref.at[pl.ds(0, TOP_K * COMBINE_TC), :], buf.at[s], sem.at[s]).wait()

    @pl.when(i == 0)
    def _():
        gather(0, 0)

    gather(jnp.minimum(i + 1, n_steps - 1), 1 - slot)
    x1b = x1b_ref[...]
    shared = _dot((jax.nn.silu(_dot(x1b, wsg_ref[...])) * _dot(x1b, wsu_ref[...])).astype(BF16), wsd_ref[...])
    wait_slot(slot)

    @pl.when(i == n_steps - 1)
    def _():
        wait_slot(1 - slot)

    wsel = wsel_ref[...]
    r_lo, r_hi = None, None
    for k in range(TOP_K):
        u = buf[slot, k * COMBINE_TC:(k + 1) * COMBINE_TC, :]
        w = wsel[:, k:k + 1]
        lo = w * lax.bitcast_convert_type(lax.shift_left(u, 16), F32)
        hi = w * lax.bitcast_convert_type(u & jnp.int32(-65536), F32)
        r_lo, r_hi = (lo, hi) if k == 0 else (r_lo + lo, r_hi + hi)
    routed = jnp.concatenate([r_lo, r_hi], axis=1)
    x2 = _layer_norm(alpha * x1_ref[...] + (routed + shared), g_ref[...], b_ref[...])
    x2_ref[...] = x2
    x2b_ref[...] = x2.astype(BF16)


def _combine(pos_flat, ys, wsel, x1, x1b, ws_gate, ws_up, ws_down, g, b, layer, alpha):
    mt, d = x1.shape
    f = ws_gate.shape[2]
    row = lambda i, pos: (i, 0)
    vec = pl.BlockSpec((None, 1, d), lambda i, pos: (layer, 0, 0))
    grid_spec = pltpu.PrefetchScalarGridSpec(
        num_scalar_prefetch=1,
        grid=(mt // COMBINE_TC,),
        in_specs=[pl.BlockSpec(memory_space=pl.ANY),
                  pl.BlockSpec((COMBINE_TC, LANES), row),
                  pl.BlockSpec((COMBINE_TC, d), row),
                  pl.BlockSpec((COMBINE_TC, d), row),
                  pl.BlockSpec((None, d, f), lambda i, pos: (layer, 0, 0)),
                  pl.BlockSpec((None, d, f), lambda i, pos: (layer, 0, 0)),
                  pl.BlockSpec((None, f, d), lambda i, pos: (layer, 0, 0)),
                  vec, vec],
        out_specs=[pl.BlockSpec((COMBINE_TC, d), row), pl.BlockSpec((COMBINE_TC, d), row)],
        scratch_shapes=[pltpu.VMEM((2, TOP_K * COMBINE_TC, d // 2), I32), pltpu.SemaphoreType.DMA((2,))])
    return pl.pallas_call(
        functools.partial(_combine_kernel, alpha=alpha, n_steps=mt // COMBINE_TC),
        grid_spec=grid_spec,
        out_shape=[jax.ShapeDtypeStruct((mt, d), F32), jax.ShapeDtypeStruct((mt, d), BF16)],
        compiler_params=_cp(("arbitrary",)),
        name="combine",
    )(pos_flat, ys, wsel, x1, x1b, ws_gate, ws_up, ws_down, g, b)


def kernel(x_prompt, x_sample, cache_k_a, cache_v_a, cache_idx_k, cache_k_c, cache_v_c, page_table, w_in, w_s, b_s, ln_v_g, ln_v_b, w_pa, w_pb, w_pc, w_out, ln1_g, ln1_b, w_router, b_router, w_gate, w_up, w_down, ws_gate, ws_up, ws_down, ln2_g, ln2_b):
    batch, seq, d = x_prompt.shape
    dec_batch, n_new, _ = x_sample.shape
    depth = w_in.shape[0]
    n_pages = page_table.shape[1]
    n_experts = w_router.shape[2]
    alpha = float((2 * depth) ** 0.25)
    mp, ms = batch * seq, dec_batch * n_new
    n_valid = mp + ms
    mt = -(-n_valid // TM) * TM
    assert seq % QB == 0 and n_new == 8 and n_experts <= LANES and mp % TM == 0
    n_blocks = -(-(n_valid * TOP_K + n_experts * (EXPERT_BLK - 1)) // EXPERT_BLK)
    n_rows = n_blocks * EXPERT_BLK

    pieces, src, off, n_cols = _layout(d)
    cols = []
    for name, w in pieces:
        s0, sw = src[name]
        piece = w_in[:, :, s0:s0 + sw]
        if sw < w:
            piece = jnp.pad(piece, ((0, 0), (0, 0), (0, w - sw)))
        cols.append(piece)
    w_in_r = jnp.concatenate(cols, axis=2).astype(BF16)

    bf = lambda a: a.astype(BF16)
    w_pa_b, w_pb_b, w_pc_b, w_out_b = bf(w_pa), bf(w_pb), bf(w_pc), bf(w_out)
    ws_gate_b, ws_up_b, ws_down_b = bf(ws_gate), bf(ws_up), bf(ws_down)
    w_r_b = bf(jnp.pad(w_router, ((0, 0), (0, 0), (0, LANES - n_experts))))
    b_r = jnp.pad(b_router, ((0, 0), (0, LANES - n_experts)))[:, None, :]
    b_s_t = jnp.swapaxes(b_s, 1, 2)
    vec3 = lambda a: a[:, None, :]
    ln1_g3, ln1_b3, ln2_g3, ln2_b3 = vec3(ln1_g), vec3(ln1_b), vec3(ln2_g), vec3(ln2_b)
    pt_flat = page_table.reshape(-1).astype(I32)
    assert n_pages % PAGE_GROUP == 0
    cache_k_c_t = jnp.swapaxes(cache_k_c, 2, 3)
    cache_v_c_t = jnp.swapaxes(cache_v_c, 2, 3)

    xf = jnp.concatenate([x_prompt.reshape(mp, d), x_sample.reshape(ms, d),
                          jnp.zeros((mt - n_valid, d), F32)], axis=0)
    xb = xf.astype(BF16)

    st = [[] for _ in range(11)]
    for l in range(depth):
        hp = _in_proj(xb, w_in_r, l)

        def cut(name, lo, hi, width):
            return hp[lo:hi, off[name]:off[name] + width]

        o_a_p = _dsa_prompt(hp, off, batch, seq)
        o_c_p = _stick_prompt(hp, off, batch, seq)
        o_b_p, _ = _gmlp(hp, hp, off["u_b"] // W_B, off["v_b"] // W_B, mp // CHUNK,
                         w_s[l], b_s_t[l], ln_v_g[l][None], ln_v_b[l][None], out_rows=mt)

        iw_s = cut("iw", mp, mp + ms, IDX_H).reshape(dec_batch, n_new, IDX_H)
        wrow = jnp.swapaxes(iw_s, 1, 2).reshape(dec_batch, 1, IDX_H * n_new)
        wrow = jnp.broadcast_to(wrow, (dec_batch, n_new, IDX_H * n_new)).reshape(ms, IDX_H * n_new)
        scores, thr = _dsa_sample_scores(hp, wrow, cache_idx_k, pt_flat, off, l, mp, dec_batch, n_new, n_pages)
        o_a_s = _dsa_sample_attend(hp, scores, thr, cache_k_a, cache_v_a, pt_flat, off, l, mp,
                                   dec_batch, n_new, n_pages)
        o_c_s = _stick_sample(hp, cache_k_c_t, cache_v_c_t, pt_flat, off, l, mp, dec_batch, n_new, n_pages)
        pad_chunk = lambda a: jnp.pad(a.reshape(dec_batch, n_new, W_B),
                                      ((0, 0), (0, CHUNK - n_new), (0, 0))).reshape(dec_batch * CHUNK, W_B)
        o_b_s, vn_s = _gmlp(pad_chunk(cut("u_b", mp, mp + ms, W_B)), pad_chunk(cut("v_b", mp, mp + ms, W_B)),
                            0, 0, dec_batch, w_s[l], b_s_t[l], ln_v_g[l][None], ln_v_b[l][None])
        unpad = lambda a: a.reshape(dec_batch, CHUNK, W_B)[:, :n_new].reshape(ms, W_B)

        o_a = lax.dynamic_update_slice(o_a_p, o_a_s.astype(BF16), (mp, 0))
        o_b = lax.dynamic_update_slice(o_b_p, unpad(o_b_s), (mp, 0))
        o_c = lax.dynamic_update_slice(o_c_p, o_c_s.astype(BF16), (mp, 0))

        m = _merge(o_a, o_b, o_c, hp, off, w_pa_b, w_pb_b, w_pc_b, l)
        x1, x1b, x1p, idx, wsel = _post_mix(m, xf, w_out_b, ln1_g3, ln1_b3, w_r_b, b_r, l, alpha, n_valid,
                                            n_experts)

        pos, be, nu = _route(idx, n_experts, n_blocks)
        pos_flat = pos[:, :TOP_K].reshape(-1)
        xs = _dispatch(pos_flat, x1p, n_rows, n_valid)
        ys = _experts(xs, be[0, :n_blocks], nu[0, :1], w_gate, w_up, w_down, l, n_blocks)
        xf, xb = _combine(pos_flat, ys, wsel, x1, x1b, ws_gate_b, ws_up_b, ws_down_b, ln2_g3, ln2_b3, l, alpha)

        kvw = KV_A * HEAD_DIM
        for lst, (name, w, shp) in zip(st[:5], (("k_a", kvw, (KV_A, HEAD_DIM)), ("v_a", kvw, (KV_A, HEAD_DIM)),
                                                ("ik", IDX_DIM, (IDX_DIM,)), ("k_c", C_W, (H_C, HEAD_DIM)),
                                                ("v_c", C_W, (H_C, HEAD_DIM)))):
            lst.append(cut(name, 0, mp, w).reshape(batch, seq, *shp))
        for lst, (name, w, shp) in zip(st[5:10], (("k_a", kvw, (KV_A, HEAD_DIM)), ("v_a", kvw, (KV_A, HEAD_DIM)),
                                                  ("ik", IDX_DIM, (IDX_DIM,)), ("k_c", C_W, (H_C, HEAD_DIM)),
                                                  ("v_c", C_W, (H_C, HEAD_DIM)))):
            lst.append(cut(name, mp, mp + ms, w).reshape(dec_batch, n_new, *shp))
        st[10].append(unpad(vn_s).reshape(dec_batch, n_new, W_B))

    y_prompt = xf[:mp].reshape(batch, seq, d)
    y_sample = xf[mp:mp + ms].reshape(dec_batch, n_new, d)
    return (y_prompt, y_sample) + tuple(jnp.stack(s) for s in st)
```
